```python
import jax
import jax.numpy as jnp
from jax import lax
import numpy as np

D_MODEL = 2048
BATCH = 4
SEQ = 8192
DEPTH = 1
DEC_BATCH = 8
DEC_SEQ = 16
PAST_LEN = 4096

CHUNK = 64
HM = 4
DQK = 128
DV = 256
HF = 8
DHF = 128
MIX = HM * DV + HF * DHF
CONV_W = 4
QK_COLS = 2 * HM * DQK
VM_COLS = HM * DV
OM_COLS = HM * DV
FOX_COLS = 3 * HF * DHF
GATE_COLS = 2 * HM + HF
P_IN = QK_COLS + VM_COLS + OM_COLS + FOX_COLS + GATE_COLS
QBLOCK = 128
N_EXPERTS = 256
TOP_K = 8
N_GROUPS = 8
TOP_GROUPS = 4
D_EXPERT = 512
D_SHARED = 512
ROUTE_SCALE = 2.5
ALPHA = (2 * DEPTH) ** 0.25
BETA = (8 * DEPTH) ** -0.25
LN_EPS = 1e-5
HEAD_EPS = 1e-6

kernel_name = 'xlstm_fox_moe_stream_step'


def _layer_norm(x, g, b):
    xf = x.astype(jnp.float32)
    mu = jnp.mean(xf, -1, keepdims=True)
    xc = xf - mu
    var = jnp.mean(xc * xc, -1, keepdims=True)
    y = xc * lax.rsqrt(var + LN_EPS) * g.astype(jnp.float32) + b.astype(jnp.float32)
    return y.astype(x.dtype)


def _mlstm_chunk(state, inp):
    C, n, m = state
    q, k, v, logi, logf = inp
    L = q.shape[2]
    b = jnp.cumsum(logf, axis=-1)
    causal = jnp.tril(jnp.ones((L, L), dtype=bool))
    dmat = jnp.where(causal, b[..., :, None] - b[..., None, :] + logi[..., None, :], -jnp.inf)
    inter = b + m[..., None]
    m_t = jnp.maximum(inter, jnp.max(dmat, axis=-1))
    w_intra = jnp.exp(dmat - m_t[..., None])
    w_state = jnp.exp(inter - m_t)
    s = jnp.einsum('bhtd,bhsd->bhts', q, k) * w_intra
    num = jnp.einsum('bhts,bhsv->bhtv', s, v) + w_state[..., None] * jnp.einsum('bhtd,bhdv->bhtv', q, C)
    den = jnp.sum(s, axis=-1) + w_state * jnp.einsum('bhtd,bhd->bht', q, n)
    h = num / jnp.maximum(jnp.abs(den), jnp.exp(-m_t))[..., None]
    b_end = b[..., -1]
    a = b_end[..., None] - b + logi
    m_new = jnp.maximum(b_end + m, jnp.max(a, axis=-1))
    decay = jnp.exp(b_end + m - m_new)
    wk = jnp.exp(a - m_new[..., None])[..., None] * k
    C_new = decay[..., None, None] * C + jnp.einsum('bhsd,bhsv->bhdv', wk, v)
    n_new = decay[..., None] * n + jnp.sum(wk, axis=2)
    return (C_new, n_new, m_new), h


def _mlstm(q, k, v, logi, logf, state, chunk):
    B, T, H, _ = q.shape
    nc = T // chunk

    def blocks(a):
        a = a.astype(jnp.float32).reshape((B, nc, chunk) + a.shape[2:])
        return jnp.swapaxes(jnp.moveaxis(a, 1, 0), 2, 3)

    state = tuple(s.astype(jnp.float32) for s in state)
    final, h = lax.scan(_mlstm_chunk, state, (blocks(q), blocks(k), blocks(v), blocks(logi), blocks(logf)))
    h = jnp.moveaxis(jnp.swapaxes(h, 2, 3), 0, 1).reshape(B, T, H, DV)
    return h, final


def _fox_prompt(q, k, v, logf):
    B, T, H, Dh = q.shape
    F = jnp.cumsum(logf.astype(jnp.float32), axis=1).transpose(0, 2, 1)
    qh = q.transpose(0, 2, 1, 3)
    kh = k.transpose(0, 2, 1, 3)
    vh = v.transpose(0, 2, 1, 3)
    kpos = jnp.arange(T)
    scale = Dh ** -0.5

    def block(i):
        start = i * QBLOCK
        qb = lax.dynamic_slice_in_dim(qh, start, QBLOCK, axis=2)
        Fb = lax.dynamic_slice_in_dim(F, start, QBLOCK, axis=2)
        s = jnp.einsum('bhqd,bhkd->bhqk', qb, kh).astype(jnp.float32) * scale
        s = s + Fb[..., None] - F[..., None, :]
        qpos = start + jnp.arange(QBLOCK)
        s = jnp.where(kpos[None, :] <= qpos[:, None], s, -jnp.inf)
        p = jax.nn.softmax(s, axis=-1)
        return jnp.einsum('bhqk,bhkd->bhqd', p.astype(vh.dtype), vh)

    o = lax.map(block, jnp.arange(T // QBLOCK))
    return o.transpose(1, 0, 3, 2, 4).reshape(B, T, H * Dh)


def _fox_sample(q, k, v, logf, ck, cv, clogf):
    B, L, H, Dh = q.shape
    P = ck.shape[1]
    k_all = jnp.concatenate([ck.astype(k.dtype), k], axis=1)
    v_all = jnp.concatenate([cv.astype(v.dtype), v], axis=1)
    F = jnp.cumsum(jnp.concatenate([clogf.astype(jnp.float32), logf.astype(jnp.float32)], axis=1), axis=1)
    F = F.transpose(0, 2, 1)
    s = jnp.einsum('bqhd,bkhd->bhqk', q, k_all).astype(jnp.float32) * Dh ** -0.5
    s = s + F[:, :, P:, None] - F[:, :, None, :]
    mask = jnp.arange(P + L)[None, :] <= (P + jnp.arange(L))[:, None]
    p = jax.nn.softmax(jnp.where(mask, s, -jnp.inf), axis=-1)
    o = jnp.einsum('bhqk,bkhd->bqhd', p.astype(v_all.dtype), v_all)
    return o.reshape(B, L, H * Dh)


def _routed_experts(x2, eidx, gw, w_e_gate, w_e_up, w_e_down, layer):
    N, D = x2.shape
    nk = N * TOP_K
    blk = 128 if nk >= N_EXPERTS * 128 else 8
    n_blocks = min(N_EXPERTS, nk) + nk // blk
    flat_e = eidx.reshape(nk)
    order = jnp.argsort(flat_e)
    se = flat_e[order]
    stok = (order // TOP_K).astype(jnp.int32)
    sw = gw.reshape(nk)[order]
    counts = jnp.bincount(flat_e, length=N_EXPERTS)
    start = jnp.cumsum(counts) - counts
    pcounts = (counts + blk - 1) // blk * blk
    pend = jnp.cumsum(pcounts)
    pstart = pend - pcounts
    dest = pstart[se] + jnp.arange(nk) - start[se]
    rows_total = n_blocks * blk
    buf_tok = jnp.full((rows_total,), N, jnp.int32).at[dest].set(stok)
    buf_w = jnp.zeros((rows_total,), jnp.float32).at[dest].set(sw)
    blk_e = jnp.minimum(jnp.searchsorted(pend, jnp.arange(n_blocks) * blk, side='right'), N_EXPERTS - 1)
    x_pad = jnp.concatenate([x2, jnp.zeros((1, D), x2.dtype)], axis=0)

    def body(acc, b):
        def run(acc):
            rows = lax.dynamic_slice_in_dim(buf_tok, b * blk, blk)
            rw = lax.dynamic_slice_in_dim(buf_w, b * blk, blk)
            e = blk_e[b]
            xb = x_pad[rows]
            hb = jax.nn.silu(xb @ w_e_gate[layer, e]) * (xb @ w_e_up[layer, e])
            yb = (hb @ w_e_down[layer, e]) * rw[:, None].astype(x2.dtype)
            return acc.at[rows].add(yb)
        return lax.cond(b * blk < pend[-1], run, lambda a: a, acc), None

    acc, _ = lax.scan(body, jnp.zeros((N + 1, D), x2.dtype), jnp.arange(n_blocks))
    return acc[:N]


def _moe(x2, w_router, router_bias, w_e_gate, w_e_up, w_e_down, layer, w_s_gate, w_s_up, w_s_down):
    N = x2.shape[0]
    scores = jax.nn.sigmoid(jnp.einsum('nd,de->ne', x2.astype(jnp.float32), w_router.astype(jnp.float32)))
    biased = scores + router_bias.astype(jnp.float32)
    grp = biased.reshape(N, N_GROUPS, N_EXPERTS // N_GROUPS)
    gscore = jnp.sum(lax.top_k(grp, 2)[0], axis=-1)
    _, gidx = lax.top_k(gscore, TOP_GROUPS)
    gmask = jnp.sum(jax.nn.one_hot(gidx, N_GROUPS, dtype=jnp.float32), axis=1) > 0
    emask = jnp.repeat(gmask, N_EXPERTS // N_GROUPS, axis=1)
    _, eidx = lax.top_k(jnp.where(emask, biased, -jnp.inf), TOP_K)
    gw = jnp.take_along_axis(scores, eidx, axis=1)
    gw = gw / jnp.sum(gw, axis=-1, keepdims=True) * ROUTE_SCALE
    routed = _routed_experts(x2, eidx, gw, w_e_gate, w_e_up, w_e_down, layer)
    shared = (jax.nn.silu(x2 @ w_s_gate) * (x2 @ w_s_up)) @ w_s_down
    return routed + shared


def _layer(x, conv_hist, C0, n0, m0, fox_past, lw, experts):
    (w_in, b_gate, conv_w, conv_b, norm_g, w_out, ln1_g, ln1_b, w_router, router_bias,
     w_s_gate, w_s_up, w_s_down, ln2_g, ln2_b) = lw
    w_e_gate, w_e_up, w_e_down, layer = experts
    B, T, D = x.shape
    dt = x.dtype
    z = jnp.einsum('btd,dp->btp', x, w_in)
    o1 = QK_COLS
    o2 = o1 + VM_COLS
    o3 = o2 + OM_COLS
    o4 = o3 + FOX_COLS
    u = jnp.concatenate([conv_hist.astype(dt), z[..., :o1]], axis=1)
    qk = lax.conv_general_dilated(u, conv_w.astype(dt)[:, None, :], (1,), 'VALID',
                                  dimension_numbers=('NWC', 'WIO', 'NWC'), feature_group_count=QK_COLS)
    qk = jax.nn.silu(qk + conv_b)
    q_m = qk[..., :HM * DQK].reshape(B, T, HM, DQK)
    k_m = qk[..., HM * DQK:].reshape(B, T, HM, DQK) * DQK ** -0.5
    v_m = z[..., o1:o2].reshape(B, T, HM, DV)
    o_m = jax.nn.sigmoid(z[..., o2:o3])
    fox = z[..., o3:o4].reshape(B, T, 3, HF, DHF)
    g = (z[..., o4:] + b_gate).astype(jnp.float32)
    logi = g[..., :HM]
    logf_m = jax.nn.log_sigmoid(g[..., HM:2 * HM])
    logf_f = jax.nn.log_sigmoid(g[..., 2 * HM:])
    h, (C1, n1, m1) = _mlstm(q_m, k_m, v_m, logi, logf_m, (C0, n0, m0), min(CHUNK, T))
    mu = jnp.mean(h, -1, keepdims=True)
    hc = h - mu
    var = jnp.mean(hc * hc, -1, keepdims=True)
    hn = (hc * lax.rsqrt(var + HEAD_EPS)).reshape(B, T, VM_COLS)
    h_m = (o_m.astype(jnp.float32) * hn * norm_g.astype(jnp.float32)).astype(dt)
    q_f, k_f, v_f = fox[:, :, 0], fox[:, :, 1], fox[:, :, 2]
    if fox_past is None:
        o_f = _fox_prompt(q_f, k_f, v_f, logf_f)
    else:
        o_f = _fox_sample(q_f, k_f, v_f, logf_f, fox_past[0], fox_past[1], fox_past[2])
    mix = jnp.einsum('btm,md->btd', jnp.concatenate([h_m, o_f.astype(dt)], axis=-1), w_out)
    h1 = _layer_norm(ALPHA * x + mix, ln1_g, ln1_b)
    f = _moe(h1.reshape(B * T, D), w_router, router_bias, w_e_gate, w_e_up, w_e_down, layer,
             w_s_gate, w_s_up, w_s_down).reshape(B, T, D)
    y = _layer_norm(ALPHA * h1 + f, ln2_g, ln2_b)
    new_state = (k_f, v_f, logf_f.astype(dt), C1.astype(dt), n1.astype(dt), m1.astype(dt),
                 u[:, -(CONV_W - 1):])
    return y, new_state


def setup_inputs(seed: int = 0) -> dict:
    key = jax.random.key(seed)
    ks = jax.random.split(key, 40)
    f32 = jnp.float32

    def nrm(k, shape, scale):
        return jax.random.normal(k, shape, f32) * scale

    x_prompt = nrm(ks[0], (BATCH, SEQ, D_MODEL), 1.0)
    x_sample = nrm(ks[1], (DEC_BATCH, DEC_SEQ, D_MODEL), 1.0)
    cache_fox_k = nrm(ks[2], (DEPTH, DEC_BATCH, PAST_LEN, HF, DHF), 1.0)
    cache_fox_v = nrm(ks[3], (DEPTH, DEC_BATCH, PAST_LEN, HF, DHF), 1.0)
    cache_fox_logf = jax.nn.log_sigmoid(4.0 + nrm(ks[4], (DEPTH, DEC_BATCH, PAST_LEN, HF), 0.5))
    state_mlstm_C = nrm(ks[5], (DEPTH, DEC_BATCH, HM, DQK, DV), 0.5)
    state_mlstm_n = jnp.abs(nrm(ks[6], (DEPTH, DEC_BATCH, HM, DQK), 0.5))
    state_mlstm_m = nrm(ks[7], (DEPTH, DEC_BATCH, HM), 1.0)
    state_conv = nrm(ks[8], (DEPTH, DEC_BATCH, CONV_W - 1, QK_COLS), 1.0)
    w_in = nrm(ks[9], (DEPTH, D_MODEL, P_IN), D_MODEL ** -0.5)
    b_i = -1.0 + nrm(ks[10], (DEPTH, HM), 0.1)
    b_fm = jnp.linspace(3.0, 6.0, HM, dtype=f32)[None, :] + nrm(ks[11], (DEPTH, HM), 0.1)
    b_ff = 4.0 + nrm(ks[12], (DEPTH, HF), 0.5)
    b_gate = jnp.concatenate([b_i, b_fm, b_ff], axis=-1)
    conv_w = nrm(ks[13], (DEPTH, CONV_W, QK_COLS), CONV_W ** -0.5)
    conv_b = nrm(ks[14], (DEPTH, QK_COLS), 0.01)
    mlstm_norm_g = 1.0 + nrm(ks[15], (DEPTH, VM_COLS), 0.02)
    w_out = nrm(ks[16], (DEPTH, MIX, D_MODEL), MIX ** -0.5 * BETA)
    ln1_g = 1.0 + nrm(ks[17], (DEPTH, D_MODEL), 0.02)
    ln1_b = nrm(ks[18], (DEPTH, D_MODEL), 0.02)
    w_router = nrm(ks[19], (DEPTH, D_MODEL, N_EXPERTS), D_MODEL ** -0.5)
    router_bias = nrm(ks[20], (DEPTH, N_EXPERTS), 0.01)
    w_e_gate = nrm(ks[21], (DEPTH, N_EXPERTS, D_MODEL, D_EXPERT), D_MODEL ** -0.5)
    w_e_up = nrm(ks[22], (DEPTH, N_EXPERTS, D_MODEL, D_EXPERT), D_MODEL ** -0.5)
    w_e_down = nrm(ks[23], (DEPTH, N_EXPERTS, D_EXPERT, D_MODEL), D_EXPERT ** -0.5 * BETA)
    w_s_gate = nrm(ks[24], (DEPTH, D_MODEL, D_SHARED), D_MODEL ** -0.5)
    w_s_up = nrm(ks[25], (DEPTH, D_MODEL, D_SHARED), D_MODEL ** -0.5)
    w_s_down = nrm(ks[26], (DEPTH, D_SHARED, D_MODEL), D_SHARED ** -0.5 * BETA)
    ln2_g = 1.0 + nrm(ks[27], (DEPTH, D_MODEL), 0.02)
    ln2_b = nrm(ks[28], (DEPTH, D_MODEL), 0.02)
    return {'x_prompt': x_prompt, 'x_sample': x_sample,
            'cache_fox_k': cache_fox_k, 'cache_fox_v': cache_fox_v, 'cache_fox_logf': cache_fox_logf,
            'state_mlstm_C': state_mlstm_C, 'state_mlstm_n': state_mlstm_n, 'state_mlstm_m': state_mlstm_m,
            'state_conv': state_conv,
            'w_in': w_in, 'b_gate': b_gate, 'conv_w': conv_w, 'conv_b': conv_b,
            'mlstm_norm_g': mlstm_norm_g, 'w_out': w_out, 'ln1_g': ln1_g, 'ln1_b': ln1_b,
            'w_router': w_router, 'router_bias': router_bias,
            'w_e_gate': w_e_gate, 'w_e_up': w_e_up, 'w_e_down': w_e_down,
            'w_s_gate': w_s_gate, 'w_s_up': w_s_up, 'w_s_down': w_s_down,
            'ln2_g': ln2_g, 'ln2_b': ln2_b}


def reference(x_prompt, x_sample, cache_fox_k, cache_fox_v, cache_fox_logf, state_mlstm_C, state_mlstm_n,
              state_mlstm_m, state_conv, w_in, b_gate, conv_w, conv_b, mlstm_norm_g, w_out, ln1_g, ln1_b,
              w_router, router_bias, w_e_gate, w_e_up, w_e_down, w_s_gate, w_s_up, w_s_down, ln2_g, ln2_b):
    bp = x_prompt.shape[0]
    dt = x_prompt.dtype
    y_p = x_prompt
    y_s = x_sample
    sp = [[] for _ in range(7)]
    ss = [[] for _ in range(7)]
    for l in range(DEPTH):
        lw = (w_in[l], b_gate[l], conv_w[l], conv_b[l], mlstm_norm_g[l], w_out[l], ln1_g[l], ln1_b[l],
              w_router[l], router_bias[l], w_s_gate[l], w_s_up[l], w_s_down[l], ln2_g[l], ln2_b[l])
        experts = (w_e_gate, w_e_up, w_e_down, l)
        y_p, st_p = _layer(y_p, jnp.zeros((bp, CONV_W - 1, QK_COLS), dt),
                           jnp.zeros((bp, HM, DQK, DV), jnp.float32), jnp.zeros((bp, HM, DQK), jnp.float32),
                           jnp.zeros((bp, HM), jnp.float32), None, lw, experts)
        y_s, st_s = _layer(y_s, state_conv[l], state_mlstm_C[l], state_mlstm_n[l], state_mlstm_m[l],
                           (cache_fox_k[l], cache_fox_v[l], cache_fox_logf[l]), lw, experts)
        for acc, s in zip(sp, st_p):
            acc.append(s)
        for acc, s in zip(ss, st_s):
            acc.append(s)
    return (y_p, y_s,
            jnp.stack(sp[0]), jnp.stack(sp[1]), jnp.stack(sp[2]), jnp.stack(sp[3]),
            jnp.stack(sp[4]), jnp.stack(sp[5]), jnp.stack(sp[6]),
            jnp.stack(ss[0]), jnp.stack(ss[1]), jnp.stack(ss[2]), jnp.stack(ss[3]),
            jnp.stack(ss[4]), jnp.stack(ss[5]), jnp.stack(ss[6]))
```

```python
import functools
import math

import jax
import jax.numpy as jnp
from jax import lax
from jax.experimental import pallas as pl
from jax.experimental.pallas import tpu as pltpu

F32 = jnp.float32
BF16 = jnp.bfloat16
I32 = jnp.int32

HM, DQK, DV = 4, 128, 256
HF, DHF = 8, 128
CONV_W = 4
QK_COLS = 2 * HM * DQK
VM_COLS = HM * DV
FOX_HEAD_COLS = HF * DHF
N_EXPERTS, TOP_K, N_GROUPS, TOP_GROUPS = 256, 8, 8, 4
GROUP_SIZE = N_EXPERTS // N_GROUPS
ROUTE_SCALE = 2.5
DEPTH = 1
ALPHA = (2 * DEPTH) ** 0.25
LN_EPS = 1e-5
HEAD_EPS = 1e-6
LOG2E = math.log2(math.e)

LANES = 128
SUBLANES = 8
VMEM_LIMIT_BYTES = 56 * 1024 * 1024

NEG_BIG = -1e30
EXPERT_BLOCK = 128


def _dot(a, b):
    return jnp.dot(a, b, preferred_element_type=F32)


def _dot_nt(a, b):
    return lax.dot_general(a, b, (((1,), (1,)), ((), ())), preferred_element_type=F32)


def _dot_tn(a, b):
    return lax.dot_general(a, b, (((0,), (0,)), ((), ())), preferred_element_type=F32)


def _split3(x):
    hi = x.astype(BF16)
    r1 = x - hi.astype(F32)
    mid = r1.astype(BF16)
    lo = (r1 - mid.astype(F32)).astype(BF16)
    return hi, mid, lo


def _split2(x):
    hi = x.astype(BF16)
    lo = (x - hi.astype(F32)).astype(BF16)
    return hi, lo


def _log_sigmoid(x):
    return jnp.minimum(x, 0.0) - jnp.log1p(jnp.exp(-jnp.abs(x)))


def _silu(x):
    return x * jax.nn.sigmoid(x)


def _cparams(sem):
    return pltpu.CompilerParams(dimension_semantics=sem, vmem_limit_bytes=VMEM_LIMIT_BYTES)


def _const_spec(shape):
    nd = len(shape)
    return pl.BlockSpec(shape, lambda *_: (0,) * nd, pipeline_mode=pl.Buffered(1))


def _inproj_kernel(x_ref, w_ref, wg_ref, wgt_ref, bg_ref, bgt_ref, *outs, head_major, q_scale):
    xb = x_ref[...].astype(BF16)
    if head_major:
        uqk_ref, vm_ref, om_ref, qf_ref, kf_ref, vf_ref, kb_ref, vb_ref, g_ref, gt_ref = outs
    else:
        uqk_ref, vm_ref, om_ref, qf_ref, kf_ref, vf_ref, g_ref, gt_ref = outs
    o = 0
    uqk_ref[...] = _dot(xb, w_ref[:, o:o + QK_COLS])
    o += QK_COLS
    vm_ref[...] = _dot(xb, w_ref[:, o:o + VM_COLS]).astype(BF16)
    o += VM_COLS
    om_ref[...] = jax.nn.sigmoid(_dot(xb, w_ref[:, o:o + VM_COLS])).astype(BF16)
    o += VM_COLS
    q = _dot(xb, w_ref[:, o:o + FOX_HEAD_COLS]) * q_scale
    o += FOX_HEAD_COLS
    k = _dot(xb, w_ref[:, o:o + FOX_HEAD_COLS])
    o += FOX_HEAD_COLS
    v = _dot(xb, w_ref[:, o:o + FOX_HEAD_COLS])
    kf_ref[...] = k
    vf_ref[...] = v
    if head_major:
        for h in range(HF):
            sl = slice(h * DHF, (h + 1) * DHF)
            qf_ref[0, h] = q[:, sl].astype(BF16)
            kb_ref[0, h] = k[:, sl].astype(BF16)
            vb_ref[0, h] = v[:, sl].astype(BF16)
    else:
        qf_ref[...] = q.astype(BF16)
    g_ref[...] = _dot(xb, wg_ref[...]) + bg_ref[...]
    gt_ref[...] = _dot_nt(wgt_ref[...], xb) + bgt_ref[...]


def _inproj(x2, w_main, w_gate, w_gate_t, b_gate_row, b_gate_col, *, tm, seq_len, head_major):
    n, d = x2.shape
    nt = n // tm
    grid = (nt,)
    row = lambda w: pl.BlockSpec((tm, w), lambda i: (i, 0))
    out_shape = [jax.ShapeDtypeStruct((n, QK_COLS), F32),
                 jax.ShapeDtypeStruct((n, VM_COLS), BF16),
                 jax.ShapeDtypeStruct((n, VM_COLS), BF16)]
    out_specs = [row(QK_COLS), row(VM_COLS), row(VM_COLS)]
    if head_major:
        bsz = n // seq_len
        tpb = seq_len // tm
        hm_shape = jax.ShapeDtypeStruct((bsz, HF, seq_len, DHF), BF16)
        hm_spec = pl.BlockSpec((1, HF, tm, DHF), lambda i: (i // tpb, 0, i % tpb, 0))
        out_shape += [hm_shape, jax.ShapeDtypeStruct((n, FOX_HEAD_COLS), F32),
                      jax.ShapeDtypeStruct((n, FOX_HEAD_COLS), F32), hm_shape, hm_shape]
        out_specs += [hm_spec, row(FOX_HEAD_COLS), row(FOX_HEAD_COLS), hm_spec, hm_spec]
    else:
        out_shape += [jax.ShapeDtypeStruct((n, FOX_HEAD_COLS), BF16),
                      jax.ShapeDtypeStruct((n, FOX_HEAD_COLS), F32),
                      jax.ShapeDtypeStruct((n, FOX_HEAD_COLS), F32)]
        out_specs += [row(FOX_HEAD_COLS), row(FOX_HEAD_COLS), row(FOX_HEAD_COLS)]
    out_shape += [jax.ShapeDtypeStruct((n, LANES), F32), jax.ShapeDtypeStruct((LANES, n), F32)]
    out_specs += [row(LANES), pl.BlockSpec((LANES, tm), lambda i: (0, i))]
    kern = functools.partial(_inproj_kernel, head_major=head_major, q_scale=DHF ** -0.5 * LOG2E)
    return pl.pallas_call(
        kern, grid=grid,
        in_specs=[row(d), _const_spec(w_main.shape), _const_spec(w_gate.shape),
                  _const_spec(w_gate_t.shape), _const_spec(b_gate_row.shape), _const_spec(b_gate_col.shape)],
        out_specs=out_specs, out_shape=out_shape,
        compiler_params=_cparams(("arbitrary",)), name="inproj",
    )(x2, w_main, w_gate, w_gate_t, b_gate_row, b_gate_col)


def _fox_cumsum_kernel(gt_ref, f_ref, lf_ref):
    lf = _log_sigmoid(gt_ref[...])
    lf_ref[...] = lf
    t = lf.shape[1]
    lane = lax.broadcasted_iota(I32, lf.shape, 1)
    x = lf
    s = 1
    while s < t:
        x = x + jnp.where(lane >= s, pltpu.roll(x, s, axis=1), 0.0)
        s *= 2
    x = x * LOG2E
    for h in range(HF):
        f_ref[0, h] = x[h:h + 1, :]


def _fox_cumsum(g_t, bsz, seq_len):
    return pl.pallas_call(
        _fox_cumsum_kernel, grid=(bsz,),
        in_specs=[pl.BlockSpec((SUBLANES, seq_len), lambda b: (1, b))],
        out_specs=[pl.BlockSpec((1, HF, 1, seq_len), lambda b: (b, 0, 0, 0)),
                   pl.BlockSpec((SUBLANES, seq_len), lambda b: (0, b))],
        out_shape=[jax.ShapeDtypeStruct((bsz, HF, 1, seq_len), F32),
                   jax.ShapeDtypeStruct((SUBLANES, bsz * seq_len), F32)],
        compiler_params=_cparams(("arbitrary",)), name="fox_cumsum",
    )(g_t)


def _mlstm_kernel(uqk_ref, v_ref, om_ref, g_ref, gt_ref, hist_ref, cw_ref, cb_ref, ng_ref,
                  c0_ref, n0_ref, m0_ref, h_ref, c1_ref, n1_ref, m1_ref,
                  c_s, n_s, m_s, prev_s):
    c_idx = pl.program_id(1)
    nc = pl.num_programs(1)
    L = uqk_ref.shape[0]

    @pl.when(c_idx == 0)
    def _():
        c_s[...] = c0_ref[0]
        n_s[...] = n0_ref[0]
        m_s[...] = m0_ref[0]
        prev_s[...] = hist_ref[0]

    z = uqk_ref[...]
    ext = jnp.concatenate([prev_s[...], z], axis=0)
    prev_s[...] = z[L - SUBLANES:, :]
    cw = cw_ref[...]
    acc = cb_ref[...] + ext[SUBLANES:SUBLANES + L] * cw[CONV_W - 1:CONV_W]
    for j in range(CONV_W - 1):
        off = SUBLANES - (CONV_W - 1) + j
        acc = acc + ext[off:off + L] * cw[j:j + 1]
    qk = _silu(acc)

    g = g_ref[...]
    gt = gt_ref[...]
    lsg = _log_sigmoid(g)
    lsgt = _log_sigmoid(gt)
    ri = lax.broadcasted_iota(I32, (L, L), 0)
    ci = lax.broadcasted_iota(I32, (L, L), 1)
    causal = ci <= ri
    tril = causal.astype(BF16)
    triu = (ri <= ci).astype(BF16)
    b_cols = sum(_dot(tril, p) for p in _split3(lsg))
    b_rows = sum(_dot(p, triu) for p in _split3(lsgt))

    for h in range(HM):
        q = qk[:, h * DQK:(h + 1) * DQK]
        k = qk[:, HM * DQK + h * DQK:HM * DQK + (h + 1) * DQK] * (DQK ** -0.5)
        qb = q.astype(BF16)
        kb = k.astype(BF16)
        v = v_ref[:, h * DV:(h + 1) * DV]
        b_col = b_cols[:, HM + h:HM + h + 1]
        b_row = b_rows[HM + h:HM + h + 1, :]
        li_col = g[:, h:h + 1]
        li_row = gt[h:h + 1, :]
        m_prev = m_s[h][:, 0:1]
        c_prev = c_s[h]
        n_prev = n_s[h]

        c_row = li_row - b_row
        dmat = jnp.where(causal, b_col + c_row, -jnp.inf)
        inter = b_col + m_prev
        m_t = jnp.maximum(inter, jnp.max(dmat, axis=1, keepdims=True))
        w_intra = jnp.exp(dmat - m_t)
        w_state = jnp.exp(inter - m_t)
        s = _dot_nt(qb, kb) * w_intra
        num = _dot(s.astype(BF16), v) + w_state * _dot(qb, c_prev.astype(BF16))
        den = jnp.sum(s, axis=1, keepdims=True) + w_state * jnp.sum(q * n_prev, axis=1, keepdims=True)
        hh = num / jnp.maximum(jnp.abs(den), jnp.exp(-m_t))

        b_end = b_col[L - 1:L, :]
        a_col = b_end - b_col + li_col
        a_row = b_end + c_row
        m_new = jnp.maximum(b_end + m_prev, jnp.max(a_row, axis=1, keepdims=True))
        decay = jnp.exp(b_end + m_prev - m_new)
        wk = jnp.exp(a_col - m_new) * k
        c_s[h] = decay * c_prev + _dot_tn(wk.astype(BF16), v)
        n_s[h] = decay * n_prev + jnp.sum(wk, axis=0, keepdims=True)
        m_s[h] = jnp.broadcast_to(m_new, (1, LANES))

        mu = jnp.mean(hh, axis=1, keepdims=True)
        hc = hh - mu
        var = jnp.mean(hc * hc, axis=1, keepdims=True)
        hn = hc * lax.rsqrt(var + HEAD_EPS)
        sl = slice(h * DV, (h + 1) * DV)
        h_ref[:, sl] = (om_ref[:, sl].astype(F32) * hn * ng_ref[:, sl]).astype(BF16)

    @pl.when(c_idx == nc - 1)
    def _():
        c1_ref[0] = c_s[...]
        n1_ref[0] = n_s[...]
        m1_ref[0] = m_s[...]


def _mlstm(uqk, vm, om, g, g_t, hist8, conv_w8, conv_b, norm_g, c0, n0, m0, *, bsz, seq_len, chunk):
    nc = seq_len // chunk
    rowspec = lambda w: pl.BlockSpec((chunk, w), lambda b, c: (b * nc + c, 0))
    st = lambda shape: pl.BlockSpec((1,) + shape, lambda b, c: (b,) + (0,) * len(shape))
    return pl.pallas_call(
        _mlstm_kernel, grid=(bsz, nc),
        in_specs=[rowspec(QK_COLS), rowspec(VM_COLS), rowspec(VM_COLS), rowspec(LANES),
                  pl.BlockSpec((SUBLANES, chunk), lambda b, c: (0, b * nc + c)),
                  st((SUBLANES, QK_COLS)),
                  pl.BlockSpec((SUBLANES, QK_COLS), lambda b, c: (0, 0)),
                  pl.BlockSpec((1, QK_COLS), lambda b, c: (0, 0)),
                  pl.BlockSpec((1, VM_COLS), lambda b, c: (0, 0)),
                  st((HM, DQK, DV)), st((HM, 1, DQK)), st((HM, 1, LANES))],
        out_specs=[rowspec(VM_COLS), st((HM, DQK, DV)), st((HM, 1, DQK)), st((HM, 1, LANES))],
        out_shape=[jax.ShapeDtypeStruct((bsz * seq_len, VM_COLS), BF16),
                   jax.ShapeDtypeStruct((bsz, HM, DQK, DV), F32),
                   jax.ShapeDtypeStruct((bsz, HM, 1, DQK), F32),
                   jax.ShapeDtypeStruct((bsz, HM, 1, LANES), F32)],
        scratch_shapes=[pltpu.VMEM((HM, DQK, DV), F32), pltpu.VMEM((HM, 1, DQK), F32),
                        pltpu.VMEM((HM, 1, LANES), F32), pltpu.VMEM((SUBLANES, QK_COLS), F32)],
        compiler_params=_cparams(("arbitrary", "arbitrary")), name="mlstm",
    )(uqk, vm, om, g, g_t, hist8, conv_w8, conv_b, norm_g, c0, n0, m0)


def _fox_prompt_kernel(q_ref, k_ref, v_ref, f_ref, o_ref, *, tq, tk):
    i = pl.program_id(2)
    q = q_ref[0, 0]
    kpb = tq // tk

    def scores(j):
        start = pl.multiple_of(j * tk, tk)
        kj = k_ref[0, 0, pl.ds(start, tk), :]
        vj = v_ref[0, 0, pl.ds(start, tk), :]
        fk = f_ref[0, 0, :, pl.ds(start, tk)]
        return _dot_nt(q, kj) - fk, vj

    def update(u, vj, carry):
        m, l, acc = carry
        m_new = jnp.maximum(m, jnp.max(u, axis=1, keepdims=True))
        alpha = jnp.exp2(m - m_new)
        p = jnp.exp2(u - m_new)
        l = alpha * l + jnp.sum(p, axis=1, keepdims=True)
        acc = alpha * acc + _dot(p.astype(BF16), vj)
        return m_new, l, acc

    def full_block(j, carry):
        u, vj = scores(j)
        return update(u, vj, carry)

    init = (jnp.full((tq, 1), -jnp.inf, F32), jnp.zeros((tq, 1), F32), jnp.zeros((tq, DHF), F32))
    carry = lax.fori_loop(0, i * kpb, full_block, init)
    qpos = lax.broadcasted_iota(I32, (tq, tk), 0)
    kpos = lax.broadcasted_iota(I32, (tq, tk), 1)
    for jj in range(kpb):
        u, vj = scores(i * kpb + jj)
        u = jnp.where(kpos + jj * tk <= qpos, u, -jnp.inf)
        carry = update(u, vj, carry)
    _, l, acc = carry
    o_ref[0] = (acc / l).astype(o_ref.dtype)


def _fox_prompt(q_hm, k_hm, v_hm, f2, *, tq, tk):
    bsz, _, seq_len, _ = q_hm.shape
    kern = functools.partial(_fox_prompt_kernel, tq=tq, tk=tk)
    kv_spec = pl.BlockSpec((1, 1, seq_len, DHF), lambda b, h, i: (b, h, 0, 0))
    return pl.pallas_call(
        kern, grid=(bsz, HF, seq_len // tq),
        in_specs=[pl.BlockSpec((1, 1, tq, DHF), lambda b, h, i: (b, h, i, 0)), kv_spec, kv_spec,
                  pl.BlockSpec((1, 1, 1, seq_len), lambda b, h, i: (b, h, 0, 0))],
        out_specs=pl.BlockSpec((1, tq, DHF), lambda b, h, i: (b, i, h)),
        out_shape=jax.ShapeDtypeStruct((bsz, seq_len, FOX_HEAD_COLS), BF16),
        compiler_params=_cparams(("arbitrary", "arbitrary", "arbitrary")), name="fox_prompt",
    )(q_hm, k_hm, v_hm, f2)


def _fox_sample_kernel(qbd_ref, ck_ref, cv_ref, clf_ref, kn_ref, vn_ref, lfn_ref, o_ref,
                       m_s, l_s, acc_s, f_s, *, kb, n_new):
    j = pl.program_id(1)
    nj = pl.num_programs(1)

    @pl.when(j == 0)
    def _():
        m_s[...] = jnp.full(m_s.shape, -jnp.inf, F32)
        l_s[...] = jnp.zeros(l_s.shape, F32)
        acc_s[...] = jnp.zeros(acc_s.shape, F32)
        f_s[...] = jnp.zeros(f_s.shape, F32)

    qbd = qbd_ref[0]
    eye = (lax.broadcasted_iota(I32, (LANES, LANES), 0) == lax.broadcasted_iota(I32, (LANES, LANES), 1))

    def to_col(row):
        return jnp.sum(jnp.where(eye, row, 0.0), axis=1, keepdims=True)

    def step(kf32, vf32, lf, valid):
        rows = kf32.shape[0]
        ri = lax.broadcasted_iota(I32, (rows, rows), 0)
        ci = lax.broadcasted_iota(I32, (rows, rows), 1)
        tril = (ci <= ri).astype(BF16)
        fk = f_s[...] + sum(_dot(tril, p) for p in _split3(lf))
        f_s[...] = fk[rows - 1:rows, :]
        u = _dot(kf32.astype(BF16), qbd) - fk * LOG2E
        if valid is not None:
            u = jnp.where(valid, u, -jnp.inf)
        m = m_s[...]
        m_new = jnp.maximum(m, jnp.max(u, axis=0, keepdims=True))
        alpha = jnp.exp2(m - m_new)
        p = jnp.exp2(u - m_new)
        l_s[...] = alpha * l_s[...] + jnp.sum(p, axis=0, keepdims=True)
        m_s[...] = m_new
        acc_s[...] = to_col(alpha) * acc_s[...] + _dot_tn(p.astype(BF16), vf32.astype(BF16))

    step(ck_ref[0], cv_ref[0], clf_ref[0], None)

    @pl.when(j == nj - 1)
    def _():
        rows = kn_ref.shape[1]
        r = lax.broadcasted_iota(I32, (rows, LANES), 0)
        t = lax.broadcasted_iota(I32, (rows, LANES), 1) % n_new
        step(kn_ref[0], vn_ref[0], lfn_ref[0], r <= t)
        inv = 1.0 / to_col(l_s[...])
        acc = acc_s[...] * inv
        for h in range(HF):
            o_ref[0, :, h * DHF:(h + 1) * DHF] = acc[h * n_new:(h + 1) * n_new,
                                                     h * DHF:(h + 1) * DHF].astype(o_ref.dtype)


def _fox_sample(qbd, ck, cv, clf_exp, k_new, v_new, lf_new_exp, *, kb, n_new):
    bsz, past, _ = ck.shape
    rows_new = k_new.shape[1]
    kern = functools.partial(_fox_sample_kernel, kb=kb, n_new=n_new)
    blk = lambda w: pl.BlockSpec((1, kb, w), lambda b, j: (b, j, 0))
    new = lambda w: pl.BlockSpec((1, rows_new, w), lambda b, j: (b, 0, 0))
    return pl.pallas_call(
        kern, grid=(bsz, past // kb),
        in_specs=[pl.BlockSpec((1, FOX_HEAD_COLS, LANES), lambda b, j: (b, 0, 0)),
                  blk(FOX_HEAD_COLS), blk(FOX_HEAD_COLS), blk(LANES),
                  new(FOX_HEAD_COLS), new(FOX_HEAD_COLS), new(LANES)],
        out_specs=pl.BlockSpec((1, n_new, FOX_HEAD_COLS), lambda b, j: (b, 0, 0)),
        out_shape=jax.ShapeDtypeStruct((bsz, n_new, FOX_HEAD_COLS), BF16),
        scratch_shapes=[pltpu.VMEM((1, LANES), F32), pltpu.VMEM((1, LANES), F32),
                        pltpu.VMEM((LANES, FOX_HEAD_COLS), F32), pltpu.VMEM((1, LANES), F32)],
        compiler_params=_cparams(("arbitrary", "arbitrary")), name="fox_sample",
    )(qbd, ck, cv, clf_exp, k_new, v_new, lf_new_exp)


def _layer_norm_rows(r, g, b):
    mu = jnp.mean(r, axis=1, keepdims=True)
    rc = r - mu
    var = jnp.mean(rc * rc, axis=1, keepdims=True)
    return rc * lax.rsqrt(var + LN_EPS) * g + b


def _outproj_router_kernel(hmp_ref, ofp_ref, xp_ref, hms_ref, ofs_ref, xs_ref,
                           wo_ref, lg_ref, lb_ref, wrh_ref, wrl_ref, rb_ref,
                           h1_ref, h1b_ref, eidx_ref, gw_ref, rank_ref, cnt_ref, cnt_s, *, ntp, n_valid):
    i = pl.program_id(0)
    tm = xp_ref.shape[0]

    @pl.when(i == 0)
    def _():
        cnt_s[...] = jnp.zeros(cnt_s.shape, F32)

    is_p = i < ntp
    hm = jnp.where(is_p, hmp_ref[...], hms_ref[...])
    of = jnp.where(is_p, ofp_ref[...], ofs_ref[...])
    x = jnp.where(is_p, xp_ref[...], xs_ref[...])
    half = hm.shape[1]
    mix = _dot(hm, wo_ref[0:half, :]) + _dot(of, wo_ref[half:, :])
    h1 = _layer_norm_rows(ALPHA * x + mix, lg_ref[...], lb_ref[...])
    h1_ref[...] = h1
    h1b_ref[...] = h1.astype(BF16)

    hh, hl = _split2(h1)
    logits = _dot_nt(wrh_ref[...], hh) + _dot_nt(wrl_ref[...], hh) + _dot_nt(wrh_ref[...], hl)
    scores = jax.nn.sigmoid(logits)
    biased = scores + rb_ref[...]

    neg = -jnp.inf
    sub = lax.broadcasted_iota(I32, (GROUP_SIZE, tm), 0).astype(F32)
    gs_rows = []
    for gi in range(N_GROUPS):
        grp = biased[gi * GROUP_SIZE:(gi + 1) * GROUP_SIZE, :]
        m1 = jnp.max(grp, axis=0, keepdims=True)
        i1 = jnp.min(jnp.where(grp == m1, sub, float(GROUP_SIZE)), axis=0, keepdims=True)
        m2 = jnp.max(jnp.where(sub == i1, neg, grp), axis=0, keepdims=True)
        gs_rows.append(m1 + m2)
    gsc = jnp.concatenate(gs_rows, axis=0)
    gid = lax.broadcasted_iota(I32, (N_GROUPS, tm), 0).astype(F32)
    gsel = jnp.zeros((N_GROUPS, tm), F32)
    for _ in range(TOP_GROUPS):
        m = jnp.max(gsc, axis=0, keepdims=True)
        idx = jnp.min(jnp.where(gsc == m, gid, float(N_GROUPS)), axis=0, keepdims=True)
        hit = gid == idx
        gsel = jnp.where(hit, 1.0, gsel)
        gsc = jnp.where(hit, neg, gsc)
    emask = jnp.concatenate(
        [jnp.broadcast_to(gsel[gi:gi + 1, :], (GROUP_SIZE, tm)) for gi in range(N_GROUPS)], axis=0)
    masked = jnp.where(emask > 0.0, biased, neg)

    eid = lax.broadcasted_iota(I32, (N_EXPERTS, tm), 0).astype(F32)
    sel = jnp.zeros((N_EXPERTS, tm), F32)
    idx_rows, gw_rows = [], []
    for _ in range(TOP_K):
        m = jnp.max(masked, axis=0, keepdims=True)
        idx = jnp.min(jnp.where(masked == m, eid, float(N_EXPERTS)), axis=0, keepdims=True)
        hit = eid == idx
        gw_rows.append(jnp.sum(jnp.where(hit, scores, 0.0), axis=0, keepdims=True))
        masked = jnp.where(hit, neg, masked)
        sel = jnp.where(hit, 1.0, sel)
        idx_rows.append(idx)
    tok = i * tm + lax.broadcasted_iota(I32, (1, tm), 1)
    sel = jnp.where(tok < n_valid, sel, 0.0)
    gsum = gw_rows[0]
    for r in gw_rows[1:]:
        gsum = gsum + r
    gscale = ROUTE_SCALE / gsum

    ri = lax.broadcasted_iota(I32, (tm, tm), 0)
    ci = lax.broadcasted_iota(I32, (tm, tm), 1)
    before = (ri < ci).astype(BF16)
    rankmat = _dot(sel.astype(BF16), before) + cnt_s[:, 0:1]
    cnt_s[...] = cnt_s[...] + jnp.sum(sel, axis=1, keepdims=True)
    for kk in range(TOP_K):
        hit = eid == idx_rows[kk]
        rk = jnp.sum(jnp.where(hit, rankmat, 0.0), axis=0, keepdims=True)
        eidx_ref[kk:kk + 1, :] = idx_rows[kk].astype(I32)
        gw_ref[kk:kk + 1, :] = gw_rows[kk] * gscale
        rank_ref[kk:kk + 1, :] = rk.astype(I32)

    @pl.when(i == pl.num_programs(0) - 1)
    def _():
        cnt_ref[...] = cnt_s[...]


def _outproj_router(hm_p, of_p, x_p, hm_s, of_s, x_s, w_out, ln_g, ln_b, wr_hi, wr_lo, rbias_col,
                    *, tm, n_valid):
    n_p, d = x_p.shape
    half = hm_p.shape[1]
    ntp = n_p // tm
    n_all = n_p + tm
    prow = lambda w: pl.BlockSpec((tm, w), lambda i: (jnp.minimum(i, ntp - 1), 0))
    srow = lambda w: pl.BlockSpec((tm, w), lambda i: (0, 0))
    orow = lambda w: pl.BlockSpec((tm, w), lambda i: (i, 0))
    trow = pl.BlockSpec((TOP_K, tm), lambda i: (0, i))
    cnt_spec = pl.BlockSpec((N_EXPERTS, LANES), lambda i: (0, 0))
    kern = functools.partial(_outproj_router_kernel, ntp=ntp, n_valid=n_valid)
    return pl.pallas_call(
        kern, grid=(ntp + 1,),
        in_specs=[prow(half), prow(half), prow(d), srow(half), srow(half), srow(d),
                  _const_spec(w_out.shape), _const_spec(ln_g.shape), _const_spec(ln_b.shape),
                  _const_spec(wr_hi.shape), _const_spec(wr_lo.shape), _const_spec(rbias_col.shape)],
        out_specs=[orow(d), orow(d), trow, trow, trow, cnt_spec],
        out_shape=[jax.ShapeDtypeStruct((n_all, d), F32), jax.ShapeDtypeStruct((n_all, d), BF16),
                   jax.ShapeDtypeStruct((TOP_K, n_all), I32), jax.ShapeDtypeStruct((TOP_K, n_all), F32),
                   jax.ShapeDtypeStruct((TOP_K, n_all), I32), jax.ShapeDtypeStruct((N_EXPERTS, LANES), F32)],
        scratch_shapes=[pltpu.VMEM((N_EXPERTS, LANES), F32)],
        compiler_params=_cparams(("arbitrary",)), name="outproj_router",
    )(hm_p, of_p, x_p, hm_s, of_s, x_s, w_out, ln_g, ln_b, wr_hi, wr_lo, rbias_col)


def _expert_kernel(blk_e_ref, nact_ref, xs_ref, wg_ref, wu_ref, wd_ref, ys_ref):
    b = pl.program_id(0)

    @pl.when(b < nact_ref[0])
    def _():
        x = xs_ref[...]
        hg = _dot(x, wg_ref[0])
        hu = _dot(x, wu_ref[0])
        hb = (_silu(hg) * hu).astype(BF16)
        ys_ref[...] = _dot(hb, wd_ref[0]).astype(ys_ref.dtype)


def _experts(xs, w_gate, w_up, w_down, blk_e, n_active):
    rows, d = xs.shape
    de = w_gate.shape[2]
    nb = rows // EXPERT_BLOCK

    def live(b, nact):
        return jnp.minimum(b, jnp.maximum(nact[0] - 1, 0))

    grid_spec = pltpu.PrefetchScalarGridSpec(
        num_scalar_prefetch=2, grid=(nb,),
        in_specs=[pl.BlockSpec((EXPERT_BLOCK, d), lambda b, be, na: (live(b, na), 0)),
                  pl.BlockSpec((1, d, de), lambda b, be, na: (be[live(b, na)], 0, 0)),
                  pl.BlockSpec((1, d, de), lambda b, be, na: (be[live(b, na)], 0, 0)),
                  pl.BlockSpec((1, de, d), lambda b, be, na: (be[live(b, na)], 0, 0))],
        out_specs=pl.BlockSpec((EXPERT_BLOCK, d), lambda b, be, na: (live(b, na), 0)))
    return pl.pallas_call(
        _expert_kernel, grid_spec=grid_spec,
        out_shape=jax.ShapeDtypeStruct((rows, d), BF16),
        compiler_params=_cparams(("arbitrary",)), name="experts",
    )(blk_e, n_active, xs, w_gate, w_up, w_down)


def _shared_ln_kernel(h1_ref, h1b_ref, routed_ref, wg_ref, wu_ref, wd_ref, lg_ref, lb_ref, yp_ref, ys_ref,
                      *, ntp):
    i = pl.program_id(0)
    xb = h1b_ref[...]
    hb = (_silu(_dot(xb, wg_ref[...])) * _dot(xb, wu_ref[...])).astype(BF16)
    f = routed_ref[...] + _dot(hb, wd_ref[...])
    y = _layer_norm_rows(ALPHA * h1_ref[...] + f, lg_ref[...], lb_ref[...])

    @pl.when(i < ntp)
    def _():
        yp_ref[...] = y

    @pl.when(i == ntp)
    def _():
        ys_ref[...] = y


def _shared_ln(h1, h1b, routed, ws_gate, ws_up, ws_down, ln_g, ln_b, *, tm):
    n_all, d = h1.shape
    ntp = n_all // tm - 1
    irow = pl.BlockSpec((tm, d), lambda i: (i, 0))
    kern = functools.partial(_shared_ln_kernel, ntp=ntp)
    return pl.pallas_call(
        kern, grid=(ntp + 1,),
        in_specs=[irow, irow, irow, _const_spec(ws_gate.shape), _const_spec(ws_up.shape),
                  _const_spec(ws_down.shape), _const_spec(ln_g.shape), _const_spec(ln_b.shape)],
        out_specs=[pl.BlockSpec((tm, d), lambda i: (jnp.minimum(i, ntp - 1), 0)),
                   pl.BlockSpec((tm, d), lambda i: (0, 0))],
        out_shape=[jax.ShapeDtypeStruct((ntp * tm, d), F32), jax.ShapeDtypeStruct((tm, d), F32)],
        compiler_params=_cparams(("arbitrary",)), name="shared_ln",
    )(h1, h1b, routed, ws_gate, ws_up, ws_down, ln_g, ln_b)


def _prep_weights(w_in, b_gate, conv_w, conv_b, norm_g, w_out, ln1_g, ln1_b, w_router, router_bias,
                  w_s_gate, w_s_up, w_s_down, ln2_g, ln2_b):
    d = w_in.shape[0]
    n_main = QK_COLS + 2 * VM_COLS + 3 * FOX_HEAD_COLS
    n_gate = w_in.shape[1] - n_main
    w_main = w_in[:, :n_main].astype(BF16)
    wg = jnp.zeros((d, LANES), F32).at[:, :n_gate].set(w_in[:, n_main:]).astype(BF16)
    bg = jnp.zeros((LANES,), F32).at[:n_gate].set(b_gate)
    wr_hi = w_router.T.astype(BF16)
    wr_lo = (w_router.T - wr_hi.astype(F32)).astype(BF16)
    return dict(
        w_main=w_main, w_gate=wg, w_gate_t=wg.T, bg_row=bg[None, :], bg_col=bg[:, None],
        conv_w8=jnp.zeros((SUBLANES, QK_COLS), F32).at[:CONV_W].set(conv_w), conv_b=conv_b[None, :],
        norm_g=norm_g[None, :], w_out=w_out.astype(BF16), ln1_g=ln1_g[None, :], ln1_b=ln1_b[None, :],
        wr_hi=wr_hi, wr_lo=wr_lo, rbias=router_bias[:, None],
        ws_gate=w_s_gate.astype(BF16), ws_up=w_s_up.astype(BF16), ws_down=w_s_down.astype(BF16),
        ln2_g=ln2_g[None, :], ln2_b=ln2_b[None, :])


def _pad_rows(a, bsz, t, tp, value=0.0):
    w = a.shape[-1]
    a = a.reshape(bsz, t, w)
    pad = jnp.full((bsz, tp - t, w), value, a.dtype) if not hasattr(value, "shape") else \
        jnp.broadcast_to(value.astype(a.dtype), (bsz, tp - t, w))
    return jnp.concatenate([a, pad], axis=1).reshape(bsz * tp, w)


def _mixer_prompt(x, wts, *, tm, chunk, tq, tk):
    bsz, t, d = x.shape
    n = bsz * t
    x2 = x.reshape(n, d)
    uqk, vm, om, qf, kf, vf, kb, vb, g, gt = _inproj(
        x2, wts["w_main"], wts["w_gate"], wts["w_gate_t"], wts["bg_row"], wts["bg_col"],
        tm=tm, seq_len=t, head_major=True)
    f2, lf_rows = _fox_cumsum(gt, bsz, t)
    zeros = lambda *s: jnp.zeros(s, F32)
    hm, c1, n1, m1 = _mlstm(uqk, vm, om, g, gt, zeros(bsz, SUBLANES, QK_COLS), wts["conv_w8"], wts["conv_b"],
                            wts["norm_g"], zeros(bsz, HM, DQK, DV), zeros(bsz, HM, 1, DQK),
                            zeros(bsz, HM, 1, LANES), bsz=bsz, seq_len=t, chunk=chunk)
    of = _fox_prompt(qf, kb, vb, f2, tq=tq, tk=tk).reshape(n, FOX_HEAD_COLS)
    state = (kf.reshape(bsz, t, HF, DHF), vf.reshape(bsz, t, HF, DHF),
             lf_rows.T.reshape(bsz, t, HF), c1, n1[:, :, 0, :], m1[:, :, 0, 0],
             uqk.reshape(bsz, t, QK_COLS)[:, t - (CONV_W - 1):, :])
    return x2, hm, of, state


def _mixer_sample(x, conv_hist, c0, n0, m0, ck, cv, clogf, wts, *, chunk, kb):
    bsz, t, d = x.shape
    n = bsz * t
    past = ck.shape[1]
    x2 = x.reshape(n, d)
    uqk, vm, om, qf, kf, vf, g, gt = _inproj(
        x2, wts["w_main"], wts["w_gate"], wts["w_gate_t"], wts["bg_row"], wts["bg_col"],
        tm=n, seq_len=t, head_major=False)
    gpad = jnp.concatenate([jnp.full((HM,), NEG_BIG, F32), jnp.full((HM,), -NEG_BIG, F32),
                            jnp.zeros((LANES - 2 * HM,), F32)])
    g_p = _pad_rows(g, bsz, t, chunk, gpad)
    hist8 = jnp.concatenate([jnp.zeros((bsz, SUBLANES - (CONV_W - 1), QK_COLS), F32), conv_hist], axis=1)
    hm_p, c1, n1, m1 = _mlstm(
        _pad_rows(uqk, bsz, t, chunk), _pad_rows(vm, bsz, t, chunk), _pad_rows(om, bsz, t, chunk),
        g_p, g_p.T, hist8, wts["conv_w8"], wts["conv_b"], wts["norm_g"],
        c0, n0[:, :, None, :], jnp.broadcast_to(m0[:, :, None, None], (bsz, HM, 1, LANES)),
        bsz=bsz, seq_len=chunk, chunk=chunk)
    hm = hm_p.reshape(bsz, chunk, VM_COLS)[:, :t].reshape(n, VM_COLS)
    lf_new = _log_sigmoid_rows(g[:, 2 * HM:2 * HM + HF])
    q4 = qf.reshape(bsz, t, HF, DHF)
    eye = jnp.eye(HF, dtype=BF16)
    qbd = jnp.einsum("bthd,hg->bhdgt", q4, eye).reshape(bsz, FOX_HEAD_COLS, HF * t)
    rows_new = LANES
    expand = lambda a: jnp.repeat(a, t, axis=-1)
    pad3 = lambda a: jnp.concatenate(
        [a, jnp.zeros((bsz, rows_new - t, a.shape[-1]), a.dtype)], axis=1)
    of = _fox_sample(qbd, ck.reshape(bsz, past, FOX_HEAD_COLS), cv.reshape(bsz, past, FOX_HEAD_COLS),
                     expand(clogf), pad3(kf.reshape(bsz, t, FOX_HEAD_COLS)),
                     pad3(vf.reshape(bsz, t, FOX_HEAD_COLS)), pad3(expand(lf_new.reshape(bsz, t, HF))),
                     kb=kb, n_new=t).reshape(n, FOX_HEAD_COLS)
    state = (kf.reshape(bsz, t, HF, DHF), vf.reshape(bsz, t, HF, DHF), lf_new.reshape(bsz, t, HF),
             c1, n1[:, :, 0, :], m1[:, :, 0, 0],
             jnp.concatenate([conv_hist, uqk.reshape(bsz, t, QK_COLS)], axis=1)[:, -(CONV_W - 1):, :])
    return x2, hm, of, state


def _log_sigmoid_rows(a):
    n, w = a.shape
    ap = jnp.zeros((n, LANES), F32).at[:, :w].set(a)

    def kern(a_ref, o_ref):
        o_ref[...] = _log_sigmoid(a_ref[...])

    out = pl.pallas_call(kern, out_shape=jax.ShapeDtypeStruct((n, LANES), F32), name="log_sigmoid")(ap)
    return out[:, :w]


def _dispatch_tables(eidx, rank, counts, n_rows_total):
    blk = EXPERT_BLOCK
    pcounts = (counts + blk - 1) // blk * blk
    pend = jnp.cumsum(pcounts)
    pstart = pend - pcounts
    dest = pstart[eidx] + rank
    nb = n_rows_total // blk
    blk_e = jnp.minimum(jnp.searchsorted(pend, jnp.arange(nb, dtype=I32) * blk, side="right"),
                        N_EXPERTS - 1).astype(I32)
    n_active = (pend[-1] // blk).astype(I32).reshape(1)
    return dest.astype(I32), blk_e, n_active


def kernel(x_prompt, x_sample, cache_fox_k, cache_fox_v, cache_fox_logf, state_mlstm_C, state_mlstm_n,
           state_mlstm_m, state_conv, w_in, b_gate, conv_w, conv_b, mlstm_norm_g, w_out, ln1_g, ln1_b,
           w_router, router_bias, w_e_gate, w_e_up, w_e_down, w_s_gate, w_s_up, w_s_down, ln2_g, ln2_b):
    l = 0
    bp, tp, d = x_prompt.shape
    bs, ts, _ = x_sample.shape
    n_p, n_s = bp * tp, bs * ts
    n_tot = n_p + n_s
    wts = _prep_weights(w_in[l], b_gate[l], conv_w[l], conv_b[l], mlstm_norm_g[l], w_out[l], ln1_g[l], ln1_b[l],
                        w_router[l], router_bias[l], w_s_gate[l], w_s_up[l], w_s_down[l], ln2_g[l], ln2_b[l])

    tm_p = min(256, tp)
    xp2, hm_p, of_p, st_p = _mixer_prompt(x_prompt, wts, tm=tm_p, chunk=min(256, tp), tq=min(512, tp),
                                          tk=min(512, tp))
    xs2, hm_s, of_s, st_s = _mixer_sample(
        x_sample, state_conv[l], state_mlstm_C[l], state_mlstm_n[l], state_mlstm_m[l],
        cache_fox_k[l], cache_fox_v[l], cache_fox_logf[l], wts, chunk=LANES, kb=min(512, cache_fox_k.shape[2]))

    tm = tm_p
    padr = lambda a: jnp.concatenate([a, jnp.zeros((tm - n_s, a.shape[1]), a.dtype)], axis=0)
    h1, h1b, eidx_t, gw_t, rank_t, cnt = _outproj_router(
        hm_p, of_p, xp2, padr(hm_s), padr(of_s), padr(xs2),
        wts["w_out"], wts["ln1_g"], wts["ln1_b"], wts["wr_hi"], wts["wr_lo"], wts["rbias"],
        tm=tm, n_valid=n_tot)
    n_all = n_p + tm
    eidx = eidx_t[:, :n_tot].T
    gw = gw_t[:, :n_tot].T
    rank = rank_t[:, :n_tot].T
    counts = cnt[:, 0].astype(I32)

    nk = n_tot * TOP_K
    n_blocks = min(N_EXPERTS, nk) + nk // EXPERT_BLOCK
    rows_total = n_blocks * EXPERT_BLOCK
    dest, blk_e, n_active = _dispatch_tables(eidx, rank, counts, rows_total)
    tok = jnp.broadcast_to(jnp.arange(n_tot, dtype=I32)[:, None], (n_tot, TOP_K))
    buf_tok = jnp.full((rows_total,), n_tot, I32).at[dest.reshape(-1)].set(tok.reshape(-1))
    xs = h1b[buf_tok]
    ys = _experts(xs, w_e_gate[l].astype(BF16), w_e_up[l].astype(BF16), w_e_down[l].astype(BF16),
                  blk_e, n_active)
    dest_all = jnp.concatenate([dest, jnp.zeros((n_all - n_tot, TOP_K), I32)], axis=0)
    gw_all = jnp.concatenate([gw, jnp.zeros((n_all - n_tot, TOP_K), F32)], axis=0)
    routed = jnp.sum(ys[dest_all].astype(F32) * gw_all[:, :, None], axis=1)

    y_p, y_s = _shared_ln(h1, h1b, routed, wts["ws_gate"], wts["ws_up"], wts["ws_down"],
                          wts["ln2_g"], wts["ln2_b"], tm=tm)
    y_p = y_p.reshape(bp, tp, d)
    y_s = y_s[:n_s].reshape(bs, ts, d)

    stack = lambda s: tuple(a[None] for a in s)
    return (y_p, y_s) + stack(st_p) + stack(st_s)
```

```python
import functools
import math

import jax
import jax.numpy as jnp
from jax import lax
from jax.experimental import pallas as pl
from jax.experimental.pallas import tpu as pltpu

F32 = jnp.float32
BF16 = jnp.bfloat16
I32 = jnp.int32

HM, DQK, DV = 4, 128, 256
HF, DHF = 8, 128
CONV_W = 4
QK_COLS = 2 * HM * DQK
VM_COLS = HM * DV
FOX_HEAD_COLS = HF * DHF
N_EXPERTS, TOP_K, N_GROUPS, TOP_GROUPS = 256, 8, 8, 4
GROUP_SIZE = N_EXPERTS // N_GROUPS
ROUTE_SCALE = 2.5
DEPTH = 1
ALPHA = (2 * DEPTH) ** 0.25
LN_EPS = 1e-5
HEAD_EPS = 1e-6
LOG2E = math.log2(math.e)

LANES = 128
SUBLANES = 8
VMEM_LIMIT_BYTES = 56 * 1024 * 1024

NEG_BIG = -1e30
EXPERT_BLOCK = 256


def _dot(a, b):
    return jnp.dot(a, b, preferred_element_type=F32)


def _dot_nt(a, b):
    return lax.dot_general(a, b, (((1,), (1,)), ((), ())), preferred_element_type=F32)


def _dot_tn(a, b):
    return lax.dot_general(a, b, (((0,), (0,)), ((), ())), preferred_element_type=F32)


def _split3(x):
    hi = x.astype(BF16)
    r1 = x - hi.astype(F32)
    mid = r1.astype(BF16)
    lo = (r1 - mid.astype(F32)).astype(BF16)
    return hi, mid, lo


def _split2(x):
    hi = x.astype(BF16)
    lo = (x - hi.astype(F32)).astype(BF16)
    return hi, lo


def _log_sigmoid(x):
    return jnp.minimum(x, 0.0) - jnp.log1p(jnp.exp(-jnp.abs(x)))


def _silu(x):
    return x * jax.nn.sigmoid(x)


def _cparams(sem):
    return pltpu.CompilerParams(dimension_semantics=sem, vmem_limit_bytes=VMEM_LIMIT_BYTES)


def _const_spec(shape):
    nd = len(shape)
    return pl.BlockSpec(shape, lambda *_: (0,) * nd, pipeline_mode=pl.Buffered(1))


def _inproj_kernel(x_ref, w_ref, wg_ref, wgt_ref, bg_ref, bgt_ref, *outs, prompt, q_scale):
    xb = x_ref[...].astype(BF16)
    if prompt:
        uqk_ref, vm_ref, om_ref, qf_ref, kf_ref, vf_ref, kb_ref, vt_ref, g_ref, gt_ref = outs
    else:
        uqk_ref, vm_ref, om_ref, qf_ref, kf_ref, vf_ref, g_ref, gt_ref = outs
    o = 0
    uqk_ref[...] = _dot(xb, w_ref[:, o:o + QK_COLS])
    o += QK_COLS
    vm_ref[...] = _dot(xb, w_ref[:, o:o + VM_COLS]).astype(BF16)
    o += VM_COLS
    om_ref[...] = jax.nn.sigmoid(_dot(xb, w_ref[:, o:o + VM_COLS])).astype(BF16)
    o += VM_COLS
    q = _dot(xb, w_ref[:, o:o + FOX_HEAD_COLS]) * q_scale
    o += FOX_HEAD_COLS
    k = _dot(xb, w_ref[:, o:o + FOX_HEAD_COLS])
    o += FOX_HEAD_COLS
    v = _dot(xb, w_ref[:, o:o + FOX_HEAD_COLS])
    kf_ref[...] = k
    vf_ref[...] = v
    if prompt:
        qf_ref[0] = q.T.astype(BF16)
        vt_ref[0] = v.T.astype(BF16)
        for h in range(HF):
            kb_ref[0, h] = k[:, h * DHF:(h + 1) * DHF].astype(BF16)
    else:
        qf_ref[...] = q.astype(BF16)
    g_ref[...] = _dot(xb, wg_ref[...]) + bg_ref[...]
    gt_ref[...] = _dot_nt(wgt_ref[...], xb) + bgt_ref[...]


def _inproj(x2, w_main, w_gate, w_gate_t, b_gate_row, b_gate_col, *, tm, seq_len, prompt):
    n, d = x2.shape
    nt = n // tm
    grid = (nt,)
    row = lambda w: pl.BlockSpec((tm, w), lambda i: (i, 0))
    out_shape = [jax.ShapeDtypeStruct((n, QK_COLS), F32),
                 jax.ShapeDtypeStruct((n, VM_COLS), BF16),
                 jax.ShapeDtypeStruct((n, VM_COLS), BF16)]
    out_specs = [row(QK_COLS), row(VM_COLS), row(VM_COLS)]
    f32_rows = jax.ShapeDtypeStruct((n, FOX_HEAD_COLS), F32)
    if prompt:
        bsz = n // seq_len
        tpb = seq_len // tm
        t_shape = jax.ShapeDtypeStruct((bsz, FOX_HEAD_COLS, seq_len), BF16)
        t_spec = pl.BlockSpec((1, FOX_HEAD_COLS, tm), lambda i: (i // tpb, 0, i % tpb))
        out_shape += [t_shape, f32_rows, f32_rows,
                      jax.ShapeDtypeStruct((bsz, HF, seq_len, DHF), BF16), t_shape]
        out_specs += [t_spec, row(FOX_HEAD_COLS), row(FOX_HEAD_COLS),
                      pl.BlockSpec((1, HF, tm, DHF), lambda i: (i // tpb, 0, i % tpb, 0)), t_spec]
    else:
        out_shape += [jax.ShapeDtypeStruct((n, FOX_HEAD_COLS), BF16), f32_rows, f32_rows]
        out_specs += [row(FOX_HEAD_COLS), row(FOX_HEAD_COLS), row(FOX_HEAD_COLS)]
    out_shape += [jax.ShapeDtypeStruct((n, LANES), F32), jax.ShapeDtypeStruct((LANES, n), F32)]
    out_specs += [row(LANES), pl.BlockSpec((LANES, tm), lambda i: (0, i))]
    kern = functools.partial(_inproj_kernel, prompt=prompt, q_scale=DHF ** -0.5 * LOG2E)
    return pl.pallas_call(
        kern, grid=grid,
        in_specs=[row(d), _const_spec(w_main.shape), _const_spec(w_gate.shape),
                  _const_spec(w_gate_t.shape), _const_spec(b_gate_row.shape), _const_spec(b_gate_col.shape)],
        out_specs=out_specs, out_shape=out_shape,
        compiler_params=_cparams(("arbitrary",)), name="inproj",
    )(x2, w_main, w_gate, w_gate_t, b_gate_row, b_gate_col)


def _fox_cumsum_kernel(g_ref, fc_ref, lf_ref, carry_s):
    @pl.when(pl.program_id(1) == 0)
    def _():
        carry_s[...] = jnp.zeros(carry_s.shape, F32)

    lf = _log_sigmoid(g_ref[...])
    lf_ref[...] = lf
    r = lf.shape[0]
    tril = (lax.broadcasted_iota(I32, (r, r), 1) <= lax.broadcasted_iota(I32, (r, r), 0)).astype(BF16)
    cs = carry_s[...] + sum(_dot(tril, p) for p in _split3(lf))
    carry_s[...] = cs[r - 1:r, :]
    fc_ref[...] = cs * LOG2E


def _fox_cumsum(g, bsz, seq_len, rows):
    nc = seq_len // rows
    spec = pl.BlockSpec((rows, LANES), lambda b, c: (b * nc + c, 0))
    shape = jax.ShapeDtypeStruct((bsz * seq_len, LANES), F32)
    return pl.pallas_call(
        _fox_cumsum_kernel, grid=(bsz, nc),
        in_specs=[spec], out_specs=[spec, spec], out_shape=[shape, shape],
        scratch_shapes=[pltpu.VMEM((1, LANES), F32)],
        compiler_params=_cparams(("arbitrary", "arbitrary")), name="fox_cumsum",
    )(g)


def _mlstm_kernel(uqk_ref, v_ref, om_ref, g_ref, gt_ref, hist_ref, cw_ref, cb_ref, ng_ref,
                  c0_ref, n0_ref, m0_ref, h_ref, c1_ref, n1_ref, m1_ref,
                  c_s, n_s, m_s, prev_s):
    c_idx = pl.program_id(1)
    nc = pl.num_programs(1)
    L = uqk_ref.shape[0]

    @pl.when(c_idx == 0)
    def _():
        c_s[...] = c0_ref[0]
        n_s[...] = n0_ref[0]
        m_s[...] = m0_ref[0]
        prev_s[...] = hist_ref[0]

    z = uqk_ref[...]
    ext = jnp.concatenate([prev_s[...], z], axis=0)
    prev_s[...] = z[L - SUBLANES:, :]
    cw = cw_ref[...]
    acc = cb_ref[...] + ext[SUBLANES:SUBLANES + L] * cw[CONV_W - 1:CONV_W]
    for j in range(CONV_W - 1):
        off = SUBLANES - (CONV_W - 1) + j
        acc = acc + ext[off:off + L] * cw[j:j + 1]
    qk = _silu(acc)

    g = g_ref[...]
    gt = gt_ref[...]
    lsg = _log_sigmoid(g)
    lsgt = _log_sigmoid(gt)
    ri = lax.broadcasted_iota(I32, (L, L), 0)
    ci = lax.broadcasted_iota(I32, (L, L), 1)
    causal = ci <= ri
    tril = causal.astype(BF16)
    triu = (ri <= ci).astype(BF16)
    b_cols = sum(_dot(tril, p) for p in _split3(lsg))
    b_rows = sum(_dot(p, triu) for p in _split3(lsgt))

    for h in range(HM):
        q = qk[:, h * DQK:(h + 1) * DQK]
        k = qk[:, HM * DQK + h * DQK:HM * DQK + (h + 1) * DQK] * (DQK ** -0.5)
        qb = q.astype(BF16)
        kb = k.astype(BF16)
        v = v_ref[:, h * DV:(h + 1) * DV]
        b_col = b_cols[:, HM + h:HM + h + 1]
        b_row = b_rows[HM + h:HM + h + 1, :]
        li_col = g[:, h:h + 1]
        li_row = gt[h:h + 1, :]
        m_prev = m_s[h][:, 0:1]
        c_prev = c_s[h]
        n_prev = n_s[h]

        c_row = li_row - b_row
        dmat = jnp.where(causal, b_col + c_row, -jnp.inf)
        inter = b_col + m_prev
        m_t = jnp.maximum(inter, jnp.max(dmat, axis=1, keepdims=True))
        w_intra = jnp.exp(dmat - m_t)
        w_state = jnp.exp(inter - m_t)
        s = _dot_nt(qb, kb) * w_intra
        num = _dot(s.astype(BF16), v) + w_state * _dot(qb, c_prev.astype(BF16))
        den = jnp.sum(s, axis=1, keepdims=True) + w_state * jnp.sum(q * n_prev, axis=1, keepdims=True)
        hh = num / jnp.maximum(jnp.abs(den), jnp.exp(-m_t))

        b_end = b_col[L - 1:L, :]
        a_col = b_end - b_col + li_col
        a_row = b_end + c_row
        m_new = jnp.maximum(b_end + m_prev, jnp.max(a_row, axis=1, keepdims=True))
        decay = jnp.exp(b_end + m_prev - m_new)
        wk = jnp.exp(a_col - m_new) * k
        c_s[h] = decay * c_prev + _dot_tn(wk.astype(BF16), v)
        n_s[h] = decay * n_prev + jnp.sum(wk, axis=0, keepdims=True)
        m_s[h] = jnp.broadcast_to(m_new, (1, LANES))

        mu = jnp.mean(hh, axis=1, keepdims=True)
        hc = hh - mu
        var = jnp.mean(hc * hc, axis=1, keepdims=True)
        hn = hc * lax.rsqrt(var + HEAD_EPS)
        sl = slice(h * DV, (h + 1) * DV)
        h_ref[:, sl] = (om_ref[:, sl].astype(F32) * hn * ng_ref[:, sl]).astype(BF16)

    @pl.when(c_idx == nc - 1)
    def _():
        c1_ref[0] = c_s[...]
        n1_ref[0] = n_s[...]
        m1_ref[0] = m_s[...]


def _mlstm(uqk, vm, om, g, g_t, hist8, conv_w8, conv_b, norm_g, c0, n0, m0, *, bsz, seq_len, chunk):
    nc = seq_len // chunk
    rowspec = lambda w: pl.BlockSpec((chunk, w), lambda b, c: (b * nc + c, 0))
    st = lambda shape: pl.BlockSpec((1,) + shape, lambda b, c: (b,) + (0,) * len(shape))
    return pl.pallas_call(
        _mlstm_kernel, grid=(bsz, nc),
        in_specs=[rowspec(QK_COLS), rowspec(VM_COLS), rowspec(VM_COLS), rowspec(LANES),
                  pl.BlockSpec((SUBLANES, chunk), lambda b, c: (0, b * nc + c)),
                  st((SUBLANES, QK_COLS)),
                  pl.BlockSpec((SUBLANES, QK_COLS), lambda b, c: (0, 0)),
                  pl.BlockSpec((1, QK_COLS), lambda b, c: (0, 0)),
                  pl.BlockSpec((1, VM_COLS), lambda b, c: (0, 0)),
                  st((HM, DQK, DV)), st((HM, 1, DQK)), st((HM, 1, LANES))],
        out_specs=[rowspec(VM_COLS), st((HM, DQK, DV)), st((HM, 1, DQK)), st((HM, 1, LANES))],
        out_shape=[jax.ShapeDtypeStruct((bsz * seq_len, VM_COLS), BF16),
                   jax.ShapeDtypeStruct((bsz, HM, DQK, DV), F32),
                   jax.ShapeDtypeStruct((bsz, HM, 1, DQK), F32),
                   jax.ShapeDtypeStruct((bsz, HM, 1, LANES), F32)],
        scratch_shapes=[pltpu.VMEM((HM, DQK, DV), F32), pltpu.VMEM((HM, 1, DQK), F32),
                        pltpu.VMEM((HM, 1, LANES), F32), pltpu.VMEM((SUBLANES, QK_COLS), F32)],
        compiler_params=_cparams(("arbitrary", "arbitrary")), name="mlstm",
    )(uqk, vm, om, g, g_t, hist8, conv_w8, conv_b, norm_g, c0, n0, m0)


FOX_GATE_COPIES = (8, 24, 40)


def _fox_prompt_kernel(qt_ref, k_ref, vt_ref, fc_ref, o_ref, kaug_s, u_s, p_s, acc_s, st_s, *, tb, prep_rows):
    h = pl.program_id(1)
    i = pl.program_id(2)
    seq = k_ref.shape[2]

    @pl.when(i == 0)
    def _():
        lane = lax.broadcasted_iota(I32, (prep_rows, LANES), 1)
        zero = jnp.zeros((prep_rows, LANES), BF16)

        def prep(c, _):
            rs = pl.ds(pl.multiple_of(c * prep_rows, prep_rows), prep_rows)
            hi, mid, lo = _split3(-fc_ref[rs, :])
            aug = jnp.where(lane == FOX_GATE_COPIES[0] + h, hi,
                            jnp.where(lane == FOX_GATE_COPIES[1] + h, mid,
                                      jnp.where(lane == FOX_GATE_COPIES[2] + h, lo, zero)))
            kaug_s[rs, 0:DHF] = k_ref[0, 0, rs, :]
            kaug_s[rs, DHF:2 * DHF] = aug
            return 0

        lax.fori_loop(0, seq // prep_rows, prep, 0)

    row = lax.broadcasted_iota(I32, (DHF, tb), 0)
    ones = jnp.where(row == FOX_GATE_COPIES[0] + h, 1.0,
                     jnp.where(row == FOX_GATE_COPIES[1] + h, 1.0,
                               jnp.where(row == FOX_GATE_COPIES[2] + h, 1.0, 0.0))).astype(BF16)
    qaug = jnp.concatenate([qt_ref[0], ones], axis=0)

    def key_rows(r):
        return pl.ds(pl.multiple_of(jnp.maximum(i - r, 0) * tb, tb), tb)

    def scores(r, slot):
        u_s[slot] = _dot(kaug_s[key_rows(r), :], qaug)

    def softmax(slot, diagonal):
        u = u_s[slot]
        if diagonal:
            kpos = lax.broadcasted_iota(I32, (tb, tb), 0)
            qpos = lax.broadcasted_iota(I32, (tb, tb), 1)
            u = jnp.where(kpos <= qpos, u, -jnp.inf)
        m = st_s[0:1, :]
        m_new = jnp.maximum(m, jnp.max(u, axis=0, keepdims=True))
        alpha = jnp.exp2(m - m_new)
        p = jnp.exp2(u - m_new)
        st_s[0:1, :] = m_new
        st_s[1:2, :] = alpha * st_s[1:2, :] + jnp.sum(p, axis=0, keepdims=True)
        st_s[2:3, :] = alpha
        p_s[slot] = p.astype(BF16)

    def accumulate(slot, r):
        acc_s[...] = st_s[2:3, :] * acc_s[...] + _dot(vt_ref[0, :, key_rows(r)], p_s[slot])

    def step(r, slot, diagonal=False, first=False, last=False):
        if not last:
            scores(r + 1, 1 - slot)
        if not first:
            accumulate(1 - slot, r - 1)
        softmax(slot, diagonal)

    def pair(s, carry):
        step(2 * s + 1, 1)
        step(2 * s + 2, 0)
        return carry

    acc_s[...] = jnp.zeros(acc_s.shape, F32)
    st_s[...] = jnp.concatenate([jnp.full((1, tb), -jnp.inf, F32), jnp.zeros((SUBLANES - 1, tb), F32)], axis=0)
    scores(0, 0)
    step(0, 0, diagonal=True, first=True)
    lax.fori_loop(0, i // 2, pair, 0)

    @pl.when(i % 2 == 1)
    def _():
        step(i, 1, last=True)
        accumulate(1, i)

    @pl.when(i % 2 == 0)
    def _():
        accumulate(0, i)

    o_ref[0] = (acc_s[...] / st_s[1:2, :]).T.astype(o_ref.dtype)


def _fox_prompt(q_t, k_hm, v_t, fcols, *, tb):
    bsz, _, seq_len, _ = k_hm.shape
    kern = functools.partial(_fox_prompt_kernel, tb=tb, prep_rows=min(512, seq_len))
    return pl.pallas_call(
        kern, grid=(bsz, HF, seq_len // tb),
        in_specs=[pl.BlockSpec((1, DHF, tb), lambda b, h, i: (b, h, i)),
                  pl.BlockSpec((1, 1, seq_len, DHF), lambda b, h, i: (b, h, 0, 0)),
                  pl.BlockSpec((1, DHF, seq_len), lambda b, h, i: (b, h, 0)),
                  pl.BlockSpec((seq_len, LANES), lambda b, h, i: (b, 0))],
        out_specs=pl.BlockSpec((1, tb, DHF), lambda b, h, i: (b, i, h)),
        out_shape=jax.ShapeDtypeStruct((bsz, seq_len, FOX_HEAD_COLS), BF16),
        scratch_shapes=[pltpu.VMEM((seq_len, 2 * DHF), BF16), pltpu.VMEM((2, tb, tb), F32),
                        pltpu.VMEM((2, tb, tb), BF16), pltpu.VMEM((DHF, tb), F32),
                        pltpu.VMEM((SUBLANES, tb), F32)],
        compiler_params=_cparams(("arbitrary", "arbitrary", "arbitrary")), name="fox_prompt",
    )(q_t, k_hm, v_t, fcols)


def _fox_sample_kernel(qbd_ref, ck_ref, cv_ref, clf_ref, kn_ref, vn_ref, lfn_ref, o_ref,
                       m_s, l_s, acc_s, f_s, *, kb, n_new):
    j = pl.program_id(1)
    nj = pl.num_programs(1)

    @pl.when(j == 0)
    def _():
        m_s[...] = jnp.full(m_s.shape, -jnp.inf, F32)
        l_s[...] = jnp.zeros(l_s.shape, F32)
        acc_s[...] = jnp.zeros(acc_s.shape, F32)
        f_s[...] = jnp.zeros(f_s.shape, F32)

    qbd = qbd_ref[0]
    eye = (lax.broadcasted_iota(I32, (LANES, LANES), 0) == lax.broadcasted_iota(I32, (LANES, LANES), 1))

    def to_col(row):
        return jnp.sum(jnp.where(eye, row, 0.0), axis=1, keepdims=True)

    def step(kf32, vf32, lf, valid):
        rows = kf32.shape[0]
        ri = lax.broadcasted_iota(I32, (rows, rows), 0)
        ci = lax.broadcasted_iota(I32, (rows, rows), 1)
        tril = (ci <= ri).astype(BF16)
        fk = f_s[...] + sum(_dot(tril, p) for p in _split3(lf))
        f_s[...] = fk[rows - 1:rows, :]
        u = _dot(kf32.astype(BF16), qbd) - fk * LOG2E
        if valid is not None:
            u = jnp.where(valid, u, -jnp.inf)
        m = m_s[...]
        m_new = jnp.maximum(m, jnp.max(u, axis=0, keepdims=True))
        alpha = jnp.exp2(m - m_new)
        p = jnp.exp2(u - m_new)
        l_s[...] = alpha * l_s[...] + jnp.sum(p, axis=0, keepdims=True)
        m_s[...] = m_new
        acc_s[...] = to_col(alpha) * acc_s[...] + _dot_tn(p.astype(BF16), vf32.astype(BF16))

    step(ck_ref[0], cv_ref[0], clf_ref[0], None)

    @pl.when(j == nj - 1)
    def _():
        rows = kn_ref.shape[1]
        r = lax.broadcasted_iota(I32, (rows, LANES), 0)
        t = lax.broadcasted_iota(I32, (rows, LANES), 1) % n_new
        step(kn_ref[0], vn_ref[0], lfn_ref[0], r <= t)
        inv = 1.0 / to_col(l_s[...])
        acc = acc_s[...] * inv
        for h in range(HF):
            o_ref[0, :, h * DHF:(h + 1) * DHF] = acc[h * n_new:(h + 1) * n_new,
                                                     h * DHF:(h + 1) * DHF].astype(o_ref.dtype)


def _fox_sample(qbd, ck, cv, clf_exp, k_new, v_new, lf_new_exp, *, kb, n_new):
    bsz, past, _ = ck.shape
    rows_new = k_new.shape[1]
    kern = functools.partial(_fox_sample_kernel, kb=kb, n_new=n_new)
    blk = lambda w: pl.BlockSpec((1, kb, w), lambda b, j: (b, j, 0))
    new = lambda w: pl.BlockSpec((1, rows_new, w), lambda b, j: (b, 0, 0))
    return pl.pallas_call(
        kern, grid=(bsz, past // kb),
        in_specs=[pl.BlockSpec((1, FOX_HEAD_COLS, LANES), lambda b, j: (b, 0, 0)),
                  blk(FOX_HEAD_COLS), blk(FOX_HEAD_COLS), blk(LANES),
                  new(FOX_HEAD_COLS), new(FOX_HEAD_COLS), new(LANES)],
        out_specs=pl.BlockSpec((1, n_new, FOX_HEAD_COLS), lambda b, j: (b, 0, 0)),
        out_shape=jax.ShapeDtypeStruct((bsz, n_new, FOX_HEAD_COLS), BF16),
        scratch_shapes=[pltpu.VMEM((1, LANES), F32), pltpu.VMEM((1, LANES), F32),
                        pltpu.VMEM((LANES, FOX_HEAD_COLS), F32), pltpu.VMEM((1, LANES), F32)],
        compiler_params=_cparams(("arbitrary", "arbitrary")), name="fox_sample",
    )(qbd, ck, cv, clf_exp, k_new, v_new, lf_new_exp)


def _layer_norm_rows(r, g, b):
    mu = jnp.mean(r, axis=1, keepdims=True)
    rc = r - mu
    var = jnp.mean(rc * rc, axis=1, keepdims=True)
    return rc * lax.rsqrt(var + LN_EPS) * g + b


def _outproj_router_kernel(hmp_ref, ofp_ref, xp_ref, hms_ref, ofs_ref, xs_ref,
                           wo_ref, lg_ref, lb_ref, wrh_ref, wrl_ref, rb_ref,
                           h1_ref, h1b_ref, eidx_ref, gw_ref, rank_ref, cnt_ref, cnt_s, *, ntp, n_valid):
    i = pl.program_id(0)
    tm = xp_ref.shape[0]

    @pl.when(i == 0)
    def _():
        cnt_s[...] = jnp.zeros(cnt_s.shape, F32)

    is_p = i < ntp
    hm = jnp.where(is_p, hmp_ref[...], hms_ref[...])
    of = jnp.where(is_p, ofp_ref[...], ofs_ref[...])
    x = jnp.where(is_p, xp_ref[...], xs_ref[...])
    half = hm.shape[1]
    mix = _dot(hm, wo_ref[0:half, :]) + _dot(of, wo_ref[half:, :])
    h1 = _layer_norm_rows(ALPHA * x + mix, lg_ref[...], lb_ref[...])
    h1_ref[...] = h1
    h1b_ref[...] = h1.astype(BF16)

    hh, hl = _split2(h1)
    logits = _dot_nt(wrh_ref[...], hh) + _dot_nt(wrl_ref[...], hh) + _dot_nt(wrh_ref[...], hl)
    scores = jax.nn.sigmoid(logits)
    biased = scores + rb_ref[...]

    neg = -jnp.inf
    sub = lax.broadcasted_iota(I32, (GROUP_SIZE, tm), 0).astype(F32)
    gs_rows = []
    for gi in range(N_GROUPS):
        grp = biased[gi * GROUP_SIZE:(gi + 1) * GROUP_SIZE, :]
        m1 = jnp.max(grp, axis=0, keepdims=True)
        i1 = jnp.min(jnp.where(grp == m1, sub, float(GROUP_SIZE)), axis=0, keepdims=True)
        m2 = jnp.max(jnp.where(sub == i1, neg, grp), axis=0, keepdims=True)
        gs_rows.append(m1 + m2)
    gsc = jnp.concatenate(gs_rows, axis=0)
    gid = lax.broadcasted_iota(I32, (N_GROUPS, tm), 0).astype(F32)
    gsel = jnp.zeros((N_GROUPS, tm), F32)
    for _ in range(TOP_GROUPS):
        m = jnp.max(gsc, axis=0, keepdims=True)
        idx = jnp.min(jnp.where(gsc == m, gid, float(N_GROUPS)), axis=0, keepdims=True)
        hit = gid == idx
        gsel = jnp.where(hit, 1.0, gsel)
        gsc = jnp.where(hit, neg, gsc)
    emask = jnp.concatenate(
        [jnp.broadcast_to(gsel[gi:gi + 1, :], (GROUP_SIZE, tm)) for gi in range(N_GROUPS)], axis=0)
    masked = jnp.where(emask > 0.0, biased, neg)

    eid = lax.broadcasted_iota(I32, (N_EXPERTS, tm), 0).astype(F32)
    sel = jnp.zeros((N_EXPERTS, tm), F32)
    idx_rows, gw_rows = [], []
    for _ in range(TOP_K):
        m = jnp.max(masked, axis=0, keepdims=True)
        idx = jnp.min(jnp.where(masked == m, eid, float(N_EXPERTS)), axis=0, keepdims=True)
        hit = eid == idx
        gw_rows.append(jnp.sum(jnp.where(hit, scores, 0.0), axis=0, keepdims=True))
        masked = jnp.where(hit, neg, masked)
        sel = jnp.where(hit, 1.0, sel)
        idx_rows.append(idx)
    tok = i * tm + lax.broadcasted_iota(I32, (1, tm), 1)
    sel = jnp.where(tok < n_valid, sel, 0.0)
    gsum = gw_rows[0]
    for r in gw_rows[1:]:
        gsum = gsum + r
    gscale = ROUTE_SCALE / gsum

    ri = lax.broadcasted_iota(I32, (tm, tm), 0)
    ci = lax.broadcasted_iota(I32, (tm, tm), 1)
    before = (ri < ci).astype(BF16)
    rankmat = _dot(sel.astype(BF16), before) + cnt_s[:, 0:1]
    cnt_s[...] = cnt_s[...] + jnp.sum(sel, axis=1, keepdims=True)
    for kk in range(TOP_K):
        hit = eid == idx_rows[kk]
        rk = jnp.sum(jnp.where(hit, rankmat, 0.0), axis=0, keepdims=True)
        eidx_ref[kk:kk + 1, :] = idx_rows[kk].astype(I32)
        gw_ref[kk:kk + 1, :] = gw_rows[kk] * gscale
        rank_ref[kk:kk + 1, :] = rk.astype(I32)

    @pl.when(i == pl.num_programs(0) - 1)
    def _():
        cnt_ref[...] = cnt_s[...]


def _outproj_router(hm_p, of_p, x_p, hm_s, of_s, x_s, w_out, ln_g, ln_b, wr_hi, wr_lo, rbias_col,
                    *, tm, n_valid):
    n_p, d = x_p.shape
    half = hm_p.shape[1]
    ntp = n_p // tm
    n_all = n_p + tm
    prow = lambda w: pl.BlockSpec((tm, w), lambda i: (jnp.minimum(i, ntp - 1), 0))
    srow = lambda w: pl.BlockSpec((tm, w), lambda i: (0, 0))
    orow = lambda w: pl.BlockSpec((tm, w), lambda i: (i, 0))
    trow = pl.BlockSpec((TOP_K, tm), lambda i: (0, i))
    cnt_spec = pl.BlockSpec((N_EXPERTS, LANES), lambda i: (0, 0))
    kern = functools.partial(_outproj_router_kernel, ntp=ntp, n_valid=n_valid)
    return pl.pallas_call(
        kern, grid=(ntp + 1,),
        in_specs=[prow(half), prow(half), prow(d), srow(half), srow(half), srow(d),
                  _const_spec(w_out.shape), _const_spec(ln_g.shape), _const_spec(ln_b.shape),
                  _const_spec(wr_hi.shape), _const_spec(wr_lo.shape), _const_spec(rbias_col.shape)],
        out_specs=[orow(d), orow(d), trow, trow, trow, cnt_spec],
        out_shape=[jax.ShapeDtypeStruct((n_all, d), F32), jax.ShapeDtypeStruct((n_all, d), BF16),
                   jax.ShapeDtypeStruct((TOP_K, n_all), I32), jax.ShapeDtypeStruct((TOP_K, n_all), F32),
                   jax.ShapeDtypeStruct((TOP_K, n_all), I32), jax.ShapeDtypeStruct((N_EXPERTS, LANES), F32)],
        scratch_shapes=[pltpu.VMEM((N_EXPERTS, LANES), F32)],
        compiler_params=_cparams(("arbitrary",)), name="outproj_router",
    )(hm_p, of_p, x_p, hm_s, of_s, x_s, w_out, ln_g, ln_b, wr_hi, wr_lo, rbias_col)


def _expert_kernel(blk_e_ref, nact_ref, next_e_ref, xs_ref, wg_hbm, wu_hbm, wd_hbm, ys_ref,
                   wg32, wu32, wd32, wgu_b, wd_b, sems, slot_ref):
    b = pl.program_id(0)
    e = blk_e_ref[b]
    de = wg32.shape[2]

    def weight_copies(expert, slot):
        return (pltpu.make_async_copy(wg_hbm.at[expert], wg32.at[slot], sems.at[slot, 0]),
                pltpu.make_async_copy(wu_hbm.at[expert], wu32.at[slot], sems.at[slot, 1]),
                pltpu.make_async_copy(wd_hbm.at[expert], wd32.at[slot], sems.at[slot, 2]))

    @pl.when(b == 0)
    def _():
        slot_ref[0] = 0
        for c in weight_copies(e, 0):
            c.start()

    active = b < nact_ref[0]
    first_of_expert = jnp.logical_or(b == 0, blk_e_ref[jnp.maximum(b - 1, 0)] != e)

    @pl.when(jnp.logical_and(active, first_of_expert))
    def _():
        slot = slot_ref[0]
        for c in weight_copies(e, slot):
            c.wait()
        wgu_b[:, 0:de] = wg32[slot].astype(BF16)
        wgu_b[:, de:2 * de] = wu32[slot].astype(BF16)
        wd_b[...] = wd32[slot].astype(BF16)
        nxt = next_e_ref[b]

        @pl.when(nxt >= 0)
        def _():
            for c in weight_copies(nxt, 1 - slot):
                c.start()

        slot_ref[0] = 1 - slot

    @pl.when(active)
    def _():
        h = _dot(xs_ref[...], wgu_b[...])
        hb = (_silu(h[:, 0:de]) * h[:, de:2 * de]).astype(BF16)
        ys_ref[...] = _dot(hb, wd_b[...]).astype(ys_ref.dtype)

    @pl.when(jnp.logical_not(active))
    def _():
        ys_ref[...] = jnp.zeros(ys_ref.shape, ys_ref.dtype)


def _experts(xs, w_gate, w_up, w_down, blk_e, n_active, next_e):
    rows, d = xs.shape
    n_exp, _, de = w_gate.shape
    nb = rows // EXPERT_BLOCK

    def live(b, nact):
        return jnp.minimum(b, jnp.maximum(nact[0] - 1, 0))

    hbm = pl.BlockSpec(memory_space=pl.ANY)
    grid_spec = pltpu.PrefetchScalarGridSpec(
        num_scalar_prefetch=3, grid=(nb,),
        in_specs=[pl.BlockSpec((EXPERT_BLOCK, d), lambda b, be, na, ne: (live(b, na), 0)), hbm, hbm, hbm],
        out_specs=pl.BlockSpec((EXPERT_BLOCK, d), lambda b, be, na, ne: (b, 0)),
        scratch_shapes=[pltpu.VMEM((2, d, de), F32), pltpu.VMEM((2, d, de), F32), pltpu.VMEM((2, de, d), F32),
                        pltpu.VMEM((d, 2 * de), BF16), pltpu.VMEM((de, d), BF16),
                        pltpu.SemaphoreType.DMA((2, 3)), pltpu.SMEM((1,), I32)])
    return pl.pallas_call(
        _expert_kernel, grid_spec=grid_spec,
        out_shape=jax.ShapeDtypeStruct((rows, d), BF16),
        compiler_params=_cparams(("arbitrary",)), name="experts",
    )(blk_e, n_active, next_e, xs, w_gate, w_up, w_down)


def _shared_ln_kernel(h1_ref, h1b_ref, routed_ref, wg_ref, wu_ref, wd_ref, lg_ref, lb_ref, yp_ref, ys_ref,
                      *, ntp):
    i = pl.program_id(0)
    xb = h1b_ref[...]
    hb = (_silu(_dot(xb, wg_ref[...])) * _dot(xb, wu_ref[...])).astype(BF16)
    f = routed_ref[...] + _dot(hb, wd_ref[...])
    y = _layer_norm_rows(ALPHA * h1_ref[...] + f, lg_ref[...], lb_ref[...])

    @pl.when(i < ntp)
    def _():
        yp_ref[...] = y

    @pl.when(i == ntp)
    def _():
        ys_ref[...] = y


def _shared_ln(h1, h1b, routed, ws_gate, ws_up, ws_down, ln_g, ln_b, *, tm):
    n_all, d = h1.shape
    ntp = n_all // tm - 1
    irow = pl.BlockSpec((tm, d), lambda i: (i, 0))
    kern = functools.partial(_shared_ln_kernel, ntp=ntp)
    return pl.pallas_call(
        kern, grid=(ntp + 1,),
        in_specs=[irow, irow, irow, _const_spec(ws_gate.shape), _const_spec(ws_up.shape),
                  _const_spec(ws_down.shape), _const_spec(ln_g.shape), _const_spec(ln_b.shape)],
        out_specs=[pl.BlockSpec((tm, d), lambda i: (jnp.minimum(i, ntp - 1), 0)),
                   pl.BlockSpec((tm, d), lambda i: (0, 0))],
        out_shape=[jax.ShapeDtypeStruct((ntp * tm, d), F32), jax.ShapeDtypeStruct((tm, d), F32)],
        compiler_params=_cparams(("arbitrary",)), name="shared_ln",
    )(h1, h1b, routed, ws_gate, ws_up, ws_down, ln_g, ln_b)


def _prep_weights(w_in, b_gate, conv_w, conv_b, norm_g, w_out, ln1_g, ln1_b, w_router, router_bias,
                  w_s_gate, w_s_up, w_s_down, ln2_g, ln2_b):
    d = w_in.shape[0]
    n_main = QK_COLS + 2 * VM_COLS + 3 * FOX_HEAD_COLS
    n_gate = w_in.shape[1] - n_main
    w_main = w_in[:, :n_main].astype(BF16)
    wg = jnp.zeros((d, LANES), F32).at[:, :n_gate].set(w_in[:, n_main:])
    bg = jnp.zeros((LANES,), F32).at[:n_gate].set(b_gate)
    for c in FOX_GATE_COPIES[1:]:
        wg = wg.at[:, c:c + HF].set(w_in[:, n_main + 2 * HM:])
        bg = bg.at[c:c + HF].set(b_gate[2 * HM:])
    wg = wg.astype(BF16)
    wr_hi = w_router.T.astype(BF16)
    wr_lo = (w_router.T - wr_hi.astype(F32)).astype(BF16)
    return dict(
        w_main=w_main, w_gate=wg, w_gate_t=wg.T, bg_row=bg[None, :], bg_col=bg[:, None],
        conv_w8=jnp.zeros((SUBLANES, QK_COLS), F32).at[:CONV_W].set(conv_w), conv_b=conv_b[None, :],
        norm_g=norm_g[None, :], w_out=w_out.astype(BF16), ln1_g=ln1_g[None, :], ln1_b=ln1_b[None, :],
        wr_hi=wr_hi, wr_lo=wr_lo, rbias=router_bias[:, None],
        ws_gate=w_s_gate.astype(BF16), ws_up=w_s_up.astype(BF16), ws_down=w_s_down.astype(BF16),
        ln2_g=ln2_g[None, :], ln2_b=ln2_b[None, :])


def _pad_rows(a, bsz, t, tp, value=0.0):
    w = a.shape[-1]
    a = a.reshape(bsz, t, w)
    pad = jnp.full((bsz, tp - t, w), value, a.dtype) if not hasattr(value, "shape") else \
        jnp.broadcast_to(value.astype(a.dtype), (bsz, tp - t, w))
    return jnp.concatenate([a, pad], axis=1).reshape(bsz * tp, w)


def _mixer_prompt(x, wts, *, tm, chunk, tb):
    bsz, t, d = x.shape
    n = bsz * t
    x2 = x.reshape(n, d)
    uqk, vm, om, qt, kf, vf, kb, vt, g, gt = _inproj(
        x2, wts["w_main"], wts["w_gate"], wts["w_gate_t"], wts["bg_row"], wts["bg_col"],
        tm=tm, seq_len=t, prompt=True)
    fcols, lf = _fox_cumsum(g, bsz, t, min(512, t))
    zeros = lambda *s: jnp.zeros(s, F32)
    hm, c1, n1, m1 = _mlstm(uqk, vm, om, g, gt, zeros(bsz, SUBLANES, QK_COLS), wts["conv_w8"], wts["conv_b"],
                            wts["norm_g"], zeros(bsz, HM, DQK, DV), zeros(bsz, HM, 1, DQK),
                            zeros(bsz, HM, 1, LANES), bsz=bsz, seq_len=t, chunk=chunk)
    of = _fox_prompt(qt, kb, vt, fcols, tb=tb).reshape(n, FOX_HEAD_COLS)
    state = (kf.reshape(bsz, t, HF, DHF), vf.reshape(bsz, t, HF, DHF),
             lf[:, 2 * HM:2 * HM + HF].reshape(bsz, t, HF), c1, n1[:, :, 0, :], m1[:, :, 0, 0],
             uqk.reshape(bsz, t, QK_COLS)[:, t - (CONV_W - 1):, :])
    return x2, hm, of, state


def _mixer_sample(x, conv_hist, c0, n0, m0, ck, cv, clogf, wts, *, chunk, kb):
    bsz, t, d = x.shape
    n = bsz * t
    past = ck.shape[1]
    x2 = x.reshape(n, d)
    uqk, vm, om, qf, kf, vf, g, gt = _inproj(
        x2, wts["w_main"], wts["w_gate"], wts["w_gate_t"], wts["bg_row"], wts["bg_col"],
        tm=n, seq_len=t, prompt=False)
    gpad = jnp.concatenate([jnp.full((HM,), NEG_BIG, F32), jnp.full((HM,), -NEG_BIG, F32),
                            jnp.zeros((LANES - 2 * HM,), F32)])
    g_p = _pad_rows(g, bsz, t, chunk, gpad)
    hist8 = jnp.concatenate([jnp.zeros((bsz, SUBLANES - (CONV_W - 1), QK_COLS), F32), conv_hist], axis=1)
    hm_p, c1, n1, m1 = _mlstm(
        _pad_rows(uqk, bsz, t, chunk), _pad_rows(vm, bsz, t, chunk), _pad_rows(om, bsz, t, chunk),
        g_p, g_p.T, hist8, wts["conv_w8"], wts["conv_b"], wts["norm_g"],
        c0, n0[:, :, None, :], jnp.broadcast_to(m0[:, :, None, None], (bsz, HM, 1, LANES)),
        bsz=bsz, seq_len=chunk, chunk=chunk)
    hm = hm_p.reshape(bsz, chunk, VM_COLS)[:, :t].reshape(n, VM_COLS)
    lf_new = _log_sigmoid_rows(g[:, 2 * HM:2 * HM + HF])
    q4 = qf.reshape(bsz, t, HF, DHF)
    eye = jnp.eye(HF, dtype=BF16)
    qbd = jnp.einsum("bthd,hg->bhdgt", q4, eye).reshape(bsz, FOX_HEAD_COLS, HF * t)
    rows_new = LANES
    expand = lambda a: jnp.repeat(a, t, axis=-1)
    pad3 = lambda a: jnp.concatenate(
        [a, jnp.zeros((bsz, rows_new - t, a.shape[-1]), a.dtype)], axis=1)
    of = _fox_sample(qbd, ck.reshape(bsz, past, FOX_HEAD_COLS), cv.reshape(bsz, past, FOX_HEAD_COLS),
                     expand(clogf), pad3(kf.reshape(bsz, t, FOX_HEAD_COLS)),
                     pad3(vf.reshape(bsz, t, FOX_HEAD_COLS)), pad3(expand(lf_new.reshape(bsz, t, HF))),
                     kb=kb, n_new=t).reshape(n, FOX_HEAD_COLS)
    state = (kf.reshape(bsz, t, HF, DHF), vf.reshape(bsz, t, HF, DHF), lf_new.reshape(bsz, t, HF),
             c1, n1[:, :, 0, :], m1[:, :, 0, 0],
             jnp.concatenate([conv_hist, uqk.reshape(bsz, t, QK_COLS)], axis=1)[:, -(CONV_W - 1):, :])
    return x2, hm, of, state


def _log_sigmoid_rows(a):
    n, w = a.shape
    ap = jnp.zeros((n, LANES), F32).at[:, :w].set(a)

    def kern(a_ref, o_ref):
        o_ref[...] = _log_sigmoid(a_ref[...])

    out = pl.pallas_call(kern, out_shape=jax.ShapeDtypeStruct((n, LANES), F32), name="log_sigmoid")(ap)
    return out[:, :w]


def _dispatch_tables(eidx, rank, counts, n_rows_total):
    blk = EXPERT_BLOCK
    pcounts = (counts + blk - 1) // blk * blk
    pend = jnp.cumsum(pcounts)
    pstart = pend - pcounts
    dest = pstart[eidx] + rank
    nb = n_rows_total // blk
    blk_e = jnp.minimum(jnp.searchsorted(pend, jnp.arange(nb, dtype=I32) * blk, side="right"),
                        N_EXPERTS - 1).astype(I32)
    n_active = (pend[-1] // blk).astype(I32)
    idx = jnp.arange(nb, dtype=I32)
    starts = jnp.concatenate([jnp.ones((1,), bool), blk_e[1:] != blk_e[:-1]])
    next_start = lax.cummin(jnp.where(starts, idx, nb), reverse=True)
    next_start = jnp.concatenate([next_start[1:], jnp.full((1,), nb, I32)])
    next_e = jnp.where(next_start < n_active, blk_e[jnp.minimum(next_start, nb - 1)], -1).astype(I32)
    return dest.astype(I32), blk_e, n_active.reshape(1), next_e


def kernel(x_prompt, x_sample, cache_fox_k, cache_fox_v, cache_fox_logf, state_mlstm_C, state_mlstm_n,
           state_mlstm_m, state_conv, w_in, b_gate, conv_w, conv_b, mlstm_norm_g, w_out, ln1_g, ln1_b,
           w_router, router_bias, w_e_gate, w_e_up, w_e_down, w_s_gate, w_s_up, w_s_down, ln2_g, ln2_b):
    l = 0
    bp, tp, d = x_prompt.shape
    bs, ts, _ = x_sample.shape
    n_p, n_s = bp * tp, bs * ts
    n_tot = n_p + n_s
    wts = _prep_weights(w_in[l], b_gate[l], conv_w[l], conv_b[l], mlstm_norm_g[l], w_out[l], ln1_g[l], ln1_b[l],
                        w_router[l], router_bias[l], w_s_gate[l], w_s_up[l], w_s_down[l], ln2_g[l], ln2_b[l])

    tm_p = min(256, tp)
    xp2, hm_p, of_p, st_p = _mixer_prompt(x_prompt, wts, tm=tm_p, chunk=min(256, tp), tb=min(512, tp))
    xs2, hm_s, of_s, st_s = _mixer_sample(
        x_sample, state_conv[l], state_mlstm_C[l], state_mlstm_n[l], state_mlstm_m[l],
        cache_fox_k[l], cache_fox_v[l], cache_fox_logf[l], wts, chunk=LANES, kb=min(512, cache_fox_k.shape[2]))

    tm = tm_p
    padr = lambda a: jnp.concatenate([a, jnp.zeros((tm - n_s, a.shape[1]), a.dtype)], axis=0)
    h1, h1b, eidx_t, gw_t, rank_t, cnt = _outproj_router(
        hm_p, of_p, xp2, padr(hm_s), padr(of_s), padr(xs2),
        wts["w_out"], wts["ln1_g"], wts["ln1_b"], wts["wr_hi"], wts["wr_lo"], wts["rbias"],
        tm=tm, n_valid=n_tot)
    n_all = n_p + tm
    eidx = eidx_t[:, :n_tot].T
    gw = gw_t[:, :n_tot].T
    rank = rank_t[:, :n_tot].T
    counts = cnt[:, 0].astype(I32)

    nk = n_tot * TOP_K
    n_blocks = min(N_EXPERTS, nk) + nk // EXPERT_BLOCK
    rows_total = n_blocks * EXPERT_BLOCK
    dest, blk_e, n_active, next_e = _dispatch_tables(eidx, rank, counts, rows_total)
    tok = jnp.broadcast_to(jnp.arange(n_tot, dtype=I32)[:, None], (n_tot, TOP_K))
    buf_tok = jnp.full((rows_total,), n_tot, I32).at[dest.reshape(-1)].set(tok.reshape(-1))
    xs = h1b[buf_tok]
    de = w_e_gate.shape[-1]
    ys = _experts(xs, w_e_gate.reshape(-1, d, de), w_e_up.reshape(-1, d, de), w_e_down.reshape(-1, de, d),
                  blk_e + l * N_EXPERTS, n_active, jnp.where(next_e >= 0, next_e + l * N_EXPERTS, -1))
    dest_all = jnp.concatenate([dest, jnp.zeros((n_all - n_tot, TOP_K), I32)], axis=0)
    gw_all = jnp.concatenate([gw, jnp.zeros((n_all - n_tot, TOP_K), F32)], axis=0)
    routed = jnp.sum(ys[dest_all].astype(F32) * gw_all[:, :, None], axis=1)

    y_p, y_s = _shared_ln(h1, h1b, routed, wts["ws_gate"], wts["ws_up"], wts["ws_down"],
                          wts["ln2_g"], wts["ln2_b"], tm=tm)
    y_p = y_p.reshape(bp, tp, d)
    y_s = y_s[:n_s].reshape(bs, ts, d)

    stack = lambda s: tuple(a[None] for a in s)
    return (y_p, y_s) + stack(st_p) + stack(st_s)
```

```python
import functools
import math

import jax
import jax.numpy as jnp
from jax import lax
from jax.experimental import pallas as pl
from jax.experimental.pallas import tpu as pltpu

F32 = jnp.float32
BF16 = jnp.bfloat16
I32 = jnp.int32
U32 = jnp.uint32

HM, DQK, DV = 4, 128, 256
HF, DHF = 8, 128
CONV_W = 4
QK_COLS = 2 * HM * DQK
VM_COLS = HM * DV
FOX_HEAD_COLS = HF * DHF
N_EXPERTS, TOP_K, N_GROUPS, TOP_GROUPS = 256, 8, 8, 4
GROUP_SIZE = N_EXPERTS // N_GROUPS
ROUTE_SCALE = 2.5
DEPTH = 1
ALPHA = (2 * DEPTH) ** 0.25
LN_EPS = 1e-5
HEAD_EPS = 1e-6
LOG2E = math.log2(math.e)

LANES = 128
SUBLANES = 8
VMEM_LIMIT_BYTES = 56 * 1024 * 1024

NEG_BIG = -1e30
EXPERT_BLOCK = 256


def _dot(a, b):
    return jnp.dot(a, b, preferred_element_type=F32)


def _dot_nt(a, b):
    return lax.dot_general(a, b, (((1,), (1,)), ((), ())), preferred_element_type=F32)


def _dot_tn(a, b):
    return lax.dot_general(a, b, (((0,), (0,)), ((), ())), preferred_element_type=F32)


def _split3(x):
    hi = x.astype(BF16)
    r1 = x - hi.astype(F32)
    mid = r1.astype(BF16)
    lo = (r1 - mid.astype(F32)).astype(BF16)
    return hi, mid, lo


def _split2(x):
    hi = x.astype(BF16)
    lo = (x - hi.astype(F32)).astype(BF16)
    return hi, lo


def _pack_halves(x):
    c = x.shape[1] // 2
    lo = lax.bitcast_convert_type(x[:, :c].astype(BF16).astype(F32), U32)
    hi = lax.bitcast_convert_type(x[:, c:].astype(BF16).astype(F32), U32)
    return (lo >> 16) | hi


def _unpack_halves(p):
    lo = lax.bitcast_convert_type(p << 16, F32)
    hi = lax.bitcast_convert_type(p & jnp.uint32(0xFFFF0000), F32)
    return jnp.concatenate([lo, hi], axis=1)


def _log_sigmoid(x):
    return jnp.minimum(x, 0.0) - jnp.log1p(jnp.exp(-jnp.abs(x)))


def _silu(x):
    return x * jax.nn.sigmoid(x)


def _cparams(sem):
    return pltpu.CompilerParams(dimension_semantics=sem, vmem_limit_bytes=VMEM_LIMIT_BYTES)


def _const_spec(shape):
    nd = len(shape)
    return pl.BlockSpec(shape, lambda *_: (0,) * nd, pipeline_mode=pl.Buffered(1))


def _inproj_kernel(x_ref, w_ref, wg_ref, wgt_ref, bg_ref, bgt_ref, *outs, prompt, q_scale):
    xb = x_ref[...].astype(BF16)
    if prompt:
        uqk_ref, vm_ref, om_ref, qf_ref, kf_ref, vf_ref, kb_ref, vt_ref, g_ref, gt_ref = outs
    else:
        uqk_ref, vm_ref, om_ref, qf_ref, kf_ref, vf_ref, g_ref, gt_ref = outs
    o = 0
    uqk_ref[...] = _dot(xb, w_ref[:, o:o + QK_COLS])
    o += QK_COLS
    vm_ref[...] = _dot(xb, w_ref[:, o:o + VM_COLS]).astype(BF16)
    o += VM_COLS
    om_ref[...] = jax.nn.sigmoid(_dot(xb, w_ref[:, o:o + VM_COLS])).astype(BF16)
    o += VM_COLS
    q = _dot(xb, w_ref[:, o:o + FOX_HEAD_COLS]) * q_scale
    o += FOX_HEAD_COLS
    k = _dot(xb, w_ref[:, o:o + FOX_HEAD_COLS])
    o += FOX_HEAD_COLS
    v = _dot(xb, w_ref[:, o:o + FOX_HEAD_COLS])
    kf_ref[...] = k
    vf_ref[...] = v
    if prompt:
        qf_ref[0] = q.T.astype(BF16)
        vt_ref[0] = v.T.astype(BF16)
        for h in range(HF):
            kb_ref[0, h] = k[:, h * DHF:(h + 1) * DHF].astype(BF16)
    else:
        qf_ref[...] = q.astype(BF16)
    g_ref[...] = _dot(xb, wg_ref[...]) + bg_ref[...]
    gt_ref[...] = _dot_nt(wgt_ref[...], xb) + bgt_ref[...]


def _inproj(x2, w_main, w_gate, w_gate_t, b_gate_row, b_gate_col, *, tm, seq_len, prompt):
    n, d = x2.shape
    nt = n // tm
    grid = (nt,)
    row = lambda w: pl.BlockSpec((tm, w), lambda i: (i, 0))
    out_shape = [jax.ShapeDtypeStruct((n, QK_COLS), F32),
                 jax.ShapeDtypeStruct((n, VM_COLS), BF16),
                 jax.ShapeDtypeStruct((n, VM_COLS), BF16)]
    out_specs = [row(QK_COLS), row(VM_COLS), row(VM_COLS)]
    f32_rows = jax.ShapeDtypeStruct((n, FOX_HEAD_COLS), F32)
    if prompt:
        bsz = n // seq_len
        tpb = seq_len // tm
        t_shape = jax.ShapeDtypeStruct((bsz, FOX_HEAD_COLS, seq_len), BF16)
        t_spec = pl.BlockSpec((1, FOX_HEAD_COLS, tm), lambda i: (i // tpb, 0, i % tpb))
        out_shape += [t_shape, f32_rows, f32_rows,
                      jax.ShapeDtypeStruct((bsz, HF, seq_len, DHF), BF16), t_shape]
        out_specs += [t_spec, row(FOX_HEAD_COLS), row(FOX_HEAD_COLS),
                      pl.BlockSpec((1, HF, tm, DHF), lambda i: (i // tpb, 0, i % tpb, 0)), t_spec]
    else:
        out_shape += [jax.ShapeDtypeStruct((n, FOX_HEAD_COLS), BF16), f32_rows, f32_rows]
        out_specs += [row(FOX_HEAD_COLS), row(FOX_HEAD_COLS), row(FOX_HEAD_COLS)]
    out_shape += [jax.ShapeDtypeStruct((n, LANES), F32), jax.ShapeDtypeStruct((LANES, n), F32)]
    out_specs += [row(LANES), pl.BlockSpec((LANES, tm), lambda i: (0, i))]
    kern = functools.partial(_inproj_kernel, prompt=prompt, q_scale=DHF ** -0.5 * LOG2E)
    return pl.pallas_call(
        kern, grid=grid,
        in_specs=[row(d), _const_spec(w_main.shape), _const_spec(w_gate.shape),
                  _const_spec(w_gate_t.shape), _const_spec(b_gate_row.shape), _const_spec(b_gate_col.shape)],
        out_specs=out_specs, out_shape=out_shape,
        compiler_params=_cparams(("arbitrary",)), name="inproj",
    )(x2, w_main, w_gate, w_gate_t, b_gate_row, b_gate_col)


def _fox_cumsum_kernel(g_ref, fc_ref, lf_ref, carry_s):
    @pl.when(pl.program_id(1) == 0)
    def _():
        carry_s[...] = jnp.zeros(carry_s.shape, F32)

    lf = _log_sigmoid(g_ref[...])
    lf_ref[...] = lf
    r = lf.shape[0]
    tril = (lax.broadcasted_iota(I32, (r, r), 1) <= lax.broadcasted_iota(I32, (r, r), 0)).astype(BF16)
    cs = carry_s[...] + sum(_dot(tril, p) for p in _split3(lf))
    carry_s[...] = cs[r - 1:r, :]
    fc_ref[...] = cs * LOG2E


def _fox_cumsum(g, bsz, seq_len, rows):
    nc = seq_len // rows
    spec = pl.BlockSpec((rows, LANES), lambda b, c: (b * nc + c, 0))
    shape = jax.ShapeDtypeStruct((bsz * seq_len, LANES), F32)
    return pl.pallas_call(
        _fox_cumsum_kernel, grid=(bsz, nc),
        in_specs=[spec], out_specs=[spec, spec], out_shape=[shape, shape],
        scratch_shapes=[pltpu.VMEM((1, LANES), F32)],
        compiler_params=_cparams(("arbitrary", "arbitrary")), name="fox_cumsum",
    )(g)


def _mlstm_kernel(uqk_ref, v_ref, om_ref, g_ref, gt_ref, hist_ref, cw_ref, cb_ref, ng_ref,
                  c0_ref, n0_ref, m0_ref, h_ref, c1_ref, n1_ref, m1_ref,
                  c_s, n_s, m_s, prev_s):
    c_idx = pl.program_id(1)
    nc = pl.num_programs(1)
    L = uqk_ref.shape[0]

    @pl.when(c_idx == 0)
    def _():
        c_s[...] = c0_ref[0]
        n_s[...] = n0_ref[0]
        m_s[...] = m0_ref[0]
        prev_s[...] = hist_ref[0]

    z = uqk_ref[...]
    ext = jnp.concatenate([prev_s[...], z], axis=0)
    prev_s[...] = z[L - SUBLANES:, :]
    cw = cw_ref[...]
    acc = cb_ref[...] + ext[SUBLANES:SUBLANES + L] * cw[CONV_W - 1:CONV_W]
    for j in range(CONV_W - 1):
        off = SUBLANES - (CONV_W - 1) + j
        acc = acc + ext[off:off + L] * cw[j:j + 1]
    qk = _silu(acc)

    g = g_ref[...]
    gt = gt_ref[...]
    lsg = _log_sigmoid(g)
    lsgt = _log_sigmoid(gt)
    ri = lax.broadcasted_iota(I32, (L, L), 0)
    ci = lax.broadcasted_iota(I32, (L, L), 1)
    causal = ci <= ri
    tril = causal.astype(BF16)
    triu = (ri <= ci).astype(BF16)
    b_cols = sum(_dot(tril, p) for p in _split3(lsg))
    b_rows = sum(_dot(p, triu) for p in _split3(lsgt))

    for h in range(HM):
        q = qk[:, h * DQK:(h + 1) * DQK]
        k = qk[:, HM * DQK + h * DQK:HM * DQK + (h + 1) * DQK] * (DQK ** -0.5)
        qb = q.astype(BF16)
        kb = k.astype(BF16)
        v = v_ref[:, h * DV:(h + 1) * DV]
        b_col = b_cols[:, HM + h:HM + h + 1]
        b_row = b_rows[HM + h:HM + h + 1, :]
        li_col = g[:, h:h + 1]
        li_row = gt[h:h + 1, :]
        m_prev = m_s[h][:, 0:1]
        c_prev = c_s[h]
        n_prev = n_s[h]

        c_row = li_row - b_row
        dmat = jnp.where(causal, b_col + c_row, -jnp.inf)
        inter = b_col + m_prev
        m_t = jnp.maximum(inter, jnp.max(dmat, axis=1, keepdims=True))
        w_intra = jnp.exp(dmat - m_t)
        w_state = jnp.exp(inter - m_t)
        s = _dot_nt(qb, kb) * w_intra
        num = _dot(s.astype(BF16), v) + w_state * _dot(qb, c_prev.astype(BF16))
        den = jnp.sum(s, axis=1, keepdims=True) + w_state * jnp.sum(q * n_prev, axis=1, keepdims=True)
        hh = num / jnp.maximum(jnp.abs(den), jnp.exp(-m_t))

        b_end = b_col[L - 1:L, :]
        a_col = b_end - b_col + li_col
        a_row = b_end + c_row
        m_new = jnp.maximum(b_end + m_prev, jnp.max(a_row, axis=1, keepdims=True))
        decay = jnp.exp(b_end + m_prev - m_new)
        wk = jnp.exp(a_col - m_new) * k
        c_s[h] = decay * c_prev + _dot_tn(wk.astype(BF16), v)
        n_s[h] = decay * n_prev + jnp.sum(wk, axis=0, keepdims=True)
        m_s[h] = jnp.broadcast_to(m_new, (1, LANES))

        mu = jnp.mean(hh, axis=1, keepdims=True)
        hc = hh - mu
        var = jnp.mean(hc * hc, axis=1, keepdims=True)
        hn = hc * lax.rsqrt(var + HEAD_EPS)
        sl = slice(h * DV, (h + 1) * DV)
        h_ref[:, sl] = (om_ref[:, sl].astype(F32) * hn * ng_ref[:, sl]).astype(BF16)

    @pl.when(c_idx == nc - 1)
    def _():
        c1_ref[0] = c_s[...]
        n1_ref[0] = n_s[...]
        m1_ref[0] = m_s[...]


def _mlstm(uqk, vm, om, g, g_t, hist8, conv_w8, conv_b, norm_g, c0, n0, m0, *, bsz, seq_len, chunk):
    nc = seq_len // chunk
    rowspec = lambda w: pl.BlockSpec((chunk, w), lambda b, c: (b * nc + c, 0))
    st = lambda shape: pl.BlockSpec((1,) + shape, lambda b, c: (b,) + (0,) * len(shape))
    return pl.pallas_call(
        _mlstm_kernel, grid=(bsz, nc),
        in_specs=[rowspec(QK_COLS), rowspec(VM_COLS), rowspec(VM_COLS), rowspec(LANES),
                  pl.BlockSpec((SUBLANES, chunk), lambda b, c: (0, b * nc + c)),
                  st((SUBLANES, QK_COLS)),
                  pl.BlockSpec((SUBLANES, QK_COLS), lambda b, c: (0, 0)),
                  pl.BlockSpec((1, QK_COLS), lambda b, c: (0, 0)),
                  pl.BlockSpec((1, VM_COLS), lambda b, c: (0, 0)),
                  st((HM, DQK, DV)), st((HM, 1, DQK)), st((HM, 1, LANES))],
        out_specs=[rowspec(VM_COLS), st((HM, DQK, DV)), st((HM, 1, DQK)), st((HM, 1, LANES))],
        out_shape=[jax.ShapeDtypeStruct((bsz * seq_len, VM_COLS), BF16),
                   jax.ShapeDtypeStruct((bsz, HM, DQK, DV), F32),
                   jax.ShapeDtypeStruct((bsz, HM, 1, DQK), F32),
                   jax.ShapeDtypeStruct((bsz, HM, 1, LANES), F32)],
        scratch_shapes=[pltpu.VMEM((HM, DQK, DV), F32), pltpu.VMEM((HM, 1, DQK), F32),
                        pltpu.VMEM((HM, 1, LANES), F32), pltpu.VMEM((SUBLANES, QK_COLS), F32)],
        compiler_params=_cparams(("arbitrary", "arbitrary")), name="mlstm",
    )(uqk, vm, om, g, g_t, hist8, conv_w8, conv_b, norm_g, c0, n0, m0)


FOX_GATE_COPIES = (8, 24, 40)


def _fox_prompt_kernel(qt_ref, k_ref, vt_ref, fc_ref, o_ref, kaug_s, u_s, p_s, acc_s, st_s, *, tb, prep_rows):
    h = pl.program_id(1)
    i = pl.program_id(2)
    seq = k_ref.shape[2]

    @pl.when(i == 0)
    def _():
        lane = lax.broadcasted_iota(I32, (prep_rows, LANES), 1)
        zero = jnp.zeros((prep_rows, LANES), BF16)

        def prep(c, _):
            rs = pl.ds(pl.multiple_of(c * prep_rows, prep_rows), prep_rows)
            hi, mid, lo = _split3(-fc_ref[rs, :])
            aug = jnp.where(lane == FOX_GATE_COPIES[0] + h, hi,
                            jnp.where(lane == FOX_GATE_COPIES[1] + h, mid,
                                      jnp.where(lane == FOX_GATE_COPIES[2] + h, lo, zero)))
            kaug_s[rs, 0:DHF] = k_ref[0, 0, rs, :]
            kaug_s[rs, DHF:2 * DHF] = aug
            return 0

        lax.fori_loop(0, seq // prep_rows, prep, 0)

    row = lax.broadcasted_iota(I32, (DHF, tb), 0)
    ones = jnp.where(row == FOX_GATE_COPIES[0] + h, 1.0,
                     jnp.where(row == FOX_GATE_COPIES[1] + h, 1.0,
                               jnp.where(row == FOX_GATE_COPIES[2] + h, 1.0, 0.0))).astype(BF16)
    qaug = jnp.concatenate([qt_ref[0], ones], axis=0)

    def key_rows(r):
        return pl.ds(pl.multiple_of(jnp.maximum(i - r, 0) * tb, tb), tb)

    def scores(r, slot):
        u_s[slot] = _dot(kaug_s[key_rows(r), :], qaug)

    def softmax(slot, diagonal):
        u = u_s[slot]
        if diagonal:
            kpos = lax.broadcasted_iota(I32, (tb, tb), 0)
            qpos = lax.broadcasted_iota(I32, (tb, tb), 1)
            u = jnp.where(kpos <= qpos, u, -jnp.inf)
        m = st_s[0:1, :]
        m_new = jnp.maximum(m, jnp.max(u, axis=0, keepdims=True))
        alpha = jnp.exp2(m - m_new)
        p = jnp.exp2(u - m_new)
        st_s[0:1, :] = m_new
        st_s[1:2, :] = alpha * st_s[1:2, :] + jnp.sum(p, axis=0, keepdims=True)
        st_s[2:3, :] = alpha
        p_s[slot] = p.astype(BF16)

    def accumulate(slot, r):
        acc_s[...] = st_s[2:3, :] * acc_s[...] + _dot(vt_ref[0, :, key_rows(r)], p_s[slot])

    def step(r, slot, diagonal=False, first=False, last=False):
        if not last:
            scores(r + 1, 1 - slot)
        if not first:
            accumulate(1 - slot, r - 1)
        softmax(slot, diagonal)

    def pair(s, carry):
        step(2 * s + 1, 1)
        step(2 * s + 2, 0)
        return carry

    acc_s[...] = jnp.zeros(acc_s.shape, F32)
    st_s[...] = jnp.concatenate([jnp.full((1, tb), -jnp.inf, F32), jnp.zeros((SUBLANES - 1, tb), F32)], axis=0)
    scores(0, 0)
    step(0, 0, diagonal=True, first=True)
    lax.fori_loop(0, i // 2, pair, 0)

    @pl.when(i % 2 == 1)
    def _():
        step(i, 1, last=True)
        accumulate(1, i)

    @pl.when(i % 2 == 0)
    def _():
        accumulate(0, i)

    o_ref[0] = (acc_s[...] / st_s[1:2, :]).T.astype(o_ref.dtype)


def _fox_prompt(q_t, k_hm, v_t, fcols, *, tb):
    bsz, _, seq_len, _ = k_hm.shape
    kern = functools.partial(_fox_prompt_kernel, tb=tb, prep_rows=min(512, seq_len))
    return pl.pallas_call(
        kern, grid=(bsz, HF, seq_len // tb),
        in_specs=[pl.BlockSpec((1, DHF, tb), lambda b, h, i: (b, h, i)),
                  pl.BlockSpec((1, 1, seq_len, DHF), lambda b, h, i: (b, h, 0, 0)),
                  pl.BlockSpec((1, DHF, seq_len), lambda b, h, i: (b, h, 0)),
                  pl.BlockSpec((seq_len, LANES), lambda b, h, i: (b, 0))],
        out_specs=pl.BlockSpec((1, tb, DHF), lambda b, h, i: (b, i, h)),
        out_shape=jax.ShapeDtypeStruct((bsz, seq_len, FOX_HEAD_COLS), BF16),
        scratch_shapes=[pltpu.VMEM((seq_len, 2 * DHF), BF16), pltpu.VMEM((2, tb, tb), F32),
                        pltpu.VMEM((2, tb, tb), BF16), pltpu.VMEM((DHF, tb), F32),
                        pltpu.VMEM((SUBLANES, tb), F32)],
        compiler_params=_cparams(("arbitrary", "arbitrary", "arbitrary")), name="fox_prompt",
    )(q_t, k_hm, v_t, fcols)


def _fox_sample_kernel(qbd_ref, ck_ref, cv_ref, clf_ref, kn_ref, vn_ref, lfn_ref, o_ref,
                       m_s, l_s, acc_s, f_s, *, kb, n_new):
    j = pl.program_id(1)
    nj = pl.num_programs(1)

    @pl.when(j == 0)
    def _():
        m_s[...] = jnp.full(m_s.shape, -jnp.inf, F32)
        l_s[...] = jnp.zeros(l_s.shape, F32)
        acc_s[...] = jnp.zeros(acc_s.shape, F32)
        f_s[...] = jnp.zeros(f_s.shape, F32)

    qbd = qbd_ref[0]
    eye = (lax.broadcasted_iota(I32, (LANES, LANES), 0) == lax.broadcasted_iota(I32, (LANES, LANES), 1))

    def to_col(row):
        return jnp.sum(jnp.where(eye, row, 0.0), axis=1, keepdims=True)

    def step(kf32, vf32, lf, valid):
        rows = kf32.shape[0]
        ri = lax.broadcasted_iota(I32, (rows, rows), 0)
        ci = lax.broadcasted_iota(I32, (rows, rows), 1)
        tril = (ci <= ri).astype(BF16)
        fk = f_s[...] + sum(_dot(tril, p) for p in _split3(lf))
        f_s[...] = fk[rows - 1:rows, :]
        u = _dot(kf32.astype(BF16), qbd) - fk * LOG2E
        if valid is not None:
            u = jnp.where(valid, u, -jnp.inf)
        m = m_s[...]
        m_new = jnp.maximum(m, jnp.max(u, axis=0, keepdims=True))
        alpha = jnp.exp2(m - m_new)
        p = jnp.exp2(u - m_new)
        l_s[...] = alpha * l_s[...] + jnp.sum(p, axis=0, keepdims=True)
        m_s[...] = m_new
        acc_s[...] = to_col(alpha) * acc_s[...] + _dot_tn(p.astype(BF16), vf32.astype(BF16))

    step(ck_ref[0], cv_ref[0], clf_ref[0], None)

    @pl.when(j == nj - 1)
    def _():
        rows = kn_ref.shape[1]
        r = lax.broadcasted_iota(I32, (rows, LANES), 0)
        t = lax.broadcasted_iota(I32, (rows, LANES), 1) % n_new
        step(kn_ref[0], vn_ref[0], lfn_ref[0], r <= t)
        inv = 1.0 / to_col(l_s[...])
        acc = acc_s[...] * inv
        for h in range(HF):
            o_ref[0, :, h * DHF:(h + 1) * DHF] = acc[h * n_new:(h + 1) * n_new,
                                                     h * DHF:(h + 1) * DHF].astype(o_ref.dtype)


def _fox_sample(qbd, ck, cv, clf_exp, k_new, v_new, lf_new_exp, *, kb, n_new):
    bsz, past, _ = ck.shape
    rows_new = k_new.shape[1]
    kern = functools.partial(_fox_sample_kernel, kb=kb, n_new=n_new)
    blk = lambda w: pl.BlockSpec((1, kb, w), lambda b, j: (b, j, 0))
    new = lambda w: pl.BlockSpec((1, rows_new, w), lambda b, j: (b, 0, 0))
    return pl.pallas_call(
        kern, grid=(bsz, past // kb),
        in_specs=[pl.BlockSpec((1, FOX_HEAD_COLS, LANES), lambda b, j: (b, 0, 0)),
                  blk(FOX_HEAD_COLS), blk(FOX_HEAD_COLS), blk(LANES),
                  new(FOX_HEAD_COLS), new(FOX_HEAD_COLS), new(LANES)],
        out_specs=pl.BlockSpec((1, n_new, FOX_HEAD_COLS), lambda b, j: (b, 0, 0)),
        out_shape=jax.ShapeDtypeStruct((bsz, n_new, FOX_HEAD_COLS), BF16),
        scratch_shapes=[pltpu.VMEM((1, LANES), F32), pltpu.VMEM((1, LANES), F32),
                        pltpu.VMEM((LANES, FOX_HEAD_COLS), F32), pltpu.VMEM((1, LANES), F32)],
        compiler_params=_cparams(("arbitrary", "arbitrary")), name="fox_sample",
    )(qbd, ck, cv, clf_exp, k_new, v_new, lf_new_exp)


def _layer_norm_rows(r, g, b):
    mu = jnp.mean(r, axis=1, keepdims=True)
    rc = r - mu
    var = jnp.mean(rc * rc, axis=1, keepdims=True)
    return rc * lax.rsqrt(var + LN_EPS) * g + b


def _outproj_router_kernel(hmp_ref, ofp_ref, xp_ref, hms_ref, ofs_ref, xs_ref,
                           wo_ref, lg_ref, lb_ref, wrh_ref, wrl_ref, rb_ref,
                           h1_ref, h1p_ref, eidx_ref, gw_ref, rank_ref, cnt_ref, cnt_s, *, ntp, n_valid):
    i = pl.program_id(0)
    tm = xp_ref.shape[0]

    @pl.when(i == 0)
    def _():
        cnt_s[...] = jnp.zeros(cnt_s.shape, F32)

    is_p = i < ntp
    hm = jnp.where(is_p, hmp_ref[...], hms_ref[...])
    of = jnp.where(is_p, ofp_ref[...], ofs_ref[...])
    x = jnp.where(is_p, xp_ref[...], xs_ref[...])
    half = hm.shape[1]
    mix = _dot(hm, wo_ref[0:half, :]) + _dot(of, wo_ref[half:, :])
    h1 = _layer_norm_rows(ALPHA * x + mix, lg_ref[...], lb_ref[...])
    h1_ref[...] = h1
    h1p_ref[...] = _pack_halves(h1)

    hh, hl = _split2(h1)
    logits = _dot_nt(wrh_ref[...], hh) + _dot_nt(wrl_ref[...], hh) + _dot_nt(wrh_ref[...], hl)
    scores = jax.nn.sigmoid(logits)
    biased = scores + rb_ref[...]

    neg = -jnp.inf
    sub = lax.broadcasted_iota(I32, (GROUP_SIZE, tm), 0).astype(F32)
    gs_rows = []
    for gi in range(N_GROUPS):
        grp = biased[gi * GROUP_SIZE:(gi + 1) * GROUP_SIZE, :]
        m1 = jnp.max(grp, axis=0, keepdims=True)
        i1 = jnp.min(jnp.where(grp == m1, sub, float(GROUP_SIZE)), axis=0, keepdims=True)
        m2 = jnp.max(jnp.where(sub == i1, neg, grp), axis=0, keepdims=True)
        gs_rows.append(m1 + m2)
    gsc = jnp.concatenate(gs_rows, axis=0)
    gid = lax.broadcasted_iota(I32, (N_GROUPS, tm), 0).astype(F32)
    gsel = jnp.zeros((N_GROUPS, tm), F32)
    for _ in range(TOP_GROUPS):
        m = jnp.max(gsc, axis=0, keepdims=True)
        idx = jnp.min(jnp.where(gsc == m, gid, float(N_GROUPS)), axis=0, keepdims=True)
        hit = gid == idx
        gsel = jnp.where(hit, 1.0, gsel)
        gsc = jnp.where(hit, neg, gsc)
    emask = jnp.concatenate(
        [jnp.broadcast_to(gsel[gi:gi + 1, :], (GROUP_SIZE, tm)) for gi in range(N_GROUPS)], axis=0)
    masked = jnp.where(emask > 0.0, biased, neg)

    eid = lax.broadcasted_iota(I32, (N_EXPERTS, tm), 0).astype(F32)
    sel = jnp.zeros((N_EXPERTS, tm), F32)
    idx_rows, gw_rows = [], []
    for _ in range(TOP_K):
        m = jnp.max(masked, axis=0, keepdims=True)
        idx = jnp.min(jnp.where(masked == m, eid, float(N_EXPERTS)), axis=0, keepdims=True)
        hit = eid == idx
        gw_rows.append(jnp.sum(jnp.where(hit, scores, 0.0), axis=0, keepdims=True))
        masked = jnp.where(hit, neg, masked)
        sel = jnp.where(hit, 1.0, sel)
        idx_rows.append(idx)
    tok = i * tm + lax.broadcasted_iota(I32, (1, tm), 1)
    sel = jnp.where(tok < n_valid, sel, 0.0)
    gsum = gw_rows[0]
    for r in gw_rows[1:]:
        gsum = gsum + r
    gscale = ROUTE_SCALE / gsum

    ri = lax.broadcasted_iota(I32, (tm, tm), 0)
    ci = lax.broadcasted_iota(I32, (tm, tm), 1)
    before = (ri < ci).astype(BF16)
    rankmat = _dot(sel.astype(BF16), before) + cnt_s[:, 0:1]
    cnt_s[...] = cnt_s[...] + jnp.sum(sel, axis=1, keepdims=True)
    for kk in range(TOP_K):
        hit = eid == idx_rows[kk]
        rk = jnp.sum(jnp.where(hit, rankmat, 0.0), axis=0, keepdims=True)
        eidx_ref[kk:kk + 1, :] = idx_rows[kk].astype(I32)
        gw_ref[kk:kk + 1, :] = gw_rows[kk] * gscale
        rank_ref[kk:kk + 1, :] = rk.astype(I32)

    @pl.when(i == pl.num_programs(0) - 1)
    def _():
        cnt_ref[...] = cnt_s[...]


def _outproj_router(hm_p, of_p, x_p, hm_s, of_s, x_s, w_out, ln_g, ln_b, wr_hi, wr_lo, rbias_col,
                    *, tm, n_valid):
    n_p, d = x_p.shape
    half = hm_p.shape[1]
    ntp = n_p // tm
    n_all = n_p + tm
    prow = lambda w: pl.BlockSpec((tm, w), lambda i: (jnp.minimum(i, ntp - 1), 0))
    srow = lambda w: pl.BlockSpec((tm, w), lambda i: (0, 0))
    orow = lambda w: pl.BlockSpec((tm, w), lambda i: (i, 0))
    trow = pl.BlockSpec((TOP_K, tm), lambda i: (0, i))
    cnt_spec = pl.BlockSpec((N_EXPERTS, LANES), lambda i: (0, 0))
    kern = functools.partial(_outproj_router_kernel, ntp=ntp, n_valid=n_valid)
    return pl.pallas_call(
        kern, grid=(ntp + 1,),
        in_specs=[prow(half), prow(half), prow(d), srow(half), srow(half), srow(d),
                  _const_spec(w_out.shape), _const_spec(ln_g.shape), _const_spec(ln_b.shape),
                  _const_spec(wr_hi.shape), _const_spec(wr_lo.shape), _const_spec(rbias_col.shape)],
        out_specs=[orow(d), orow(d // 2), trow, trow, trow, cnt_spec],
        out_shape=[jax.ShapeDtypeStruct((n_all, d), F32), jax.ShapeDtypeStruct((n_all, d // 2), U32),
                   jax.ShapeDtypeStruct((TOP_K, n_all), I32), jax.ShapeDtypeStruct((TOP_K, n_all), F32),
                   jax.ShapeDtypeStruct((TOP_K, n_all), I32), jax.ShapeDtypeStruct((N_EXPERTS, LANES), F32)],
        scratch_shapes=[pltpu.VMEM((N_EXPERTS, LANES), F32)],
        compiler_params=_cparams(("arbitrary",)), name="outproj_router",
    )(hm_p, of_p, x_p, hm_s, of_s, x_s, w_out, ln_g, ln_b, wr_hi, wr_lo, rbias_col)


def _dispatch_kernel(dest_ref, zflag_ref, h_ref, xs_hbm, zeros_s, sem, zsem, *, tm):
    i = pl.program_id(0)
    nblk = zflag_ref.shape[0]

    def zero_copy(b):
        rows = pl.ds(pl.multiple_of(b * EXPERT_BLOCK, EXPERT_BLOCK), EXPERT_BLOCK)
        return pltpu.make_async_copy(zeros_s, xs_hbm.at[rows, :], zsem)

    @pl.when(i == 0)
    def _():
        zeros_s[...] = jnp.zeros(zeros_s.shape, zeros_s.dtype)

        def start(b, _):
            @pl.when(zflag_ref[b] != 0)
            def _():
                zero_copy(b).start()
            return 0

        def wait(b, _):
            @pl.when(zflag_ref[b] != 0)
            def _():
                zero_copy(b).wait()
            return 0

        lax.fori_loop(0, nblk, start, 0)
        lax.fori_loop(0, nblk, wait, 0)

    def issue(t, _):
        for k in range(TOP_K):
            d = dest_ref[t * TOP_K + k]
            pltpu.make_async_copy(h_ref.at[pl.ds(t, 1), :], xs_hbm.at[pl.ds(d, 1), :], sem).start()
        return 0

    lax.fori_loop(0, tm, issue, 0)
    for _ in range(TOP_K):
        pltpu.make_async_copy(h_ref, xs_hbm.at[pl.ds(0, tm), :], sem).wait()


def _dispatch(h1p, dest_flat, zflag, *, tm, rows_alloc):
    n_all, c = h1p.shape
    kern = functools.partial(_dispatch_kernel, tm=tm)
    return pl.pallas_call(
        kern, grid=(n_all // tm,),
        in_specs=[pl.BlockSpec((tm * TOP_K,), lambda i: (i,), memory_space=pltpu.SMEM),
                  pl.BlockSpec(memory_space=pltpu.SMEM),
                  pl.BlockSpec((tm, c), lambda i: (i, 0))],
        out_specs=pl.BlockSpec(memory_space=pl.ANY),
        out_shape=jax.ShapeDtypeStruct((rows_alloc, c), U32),
        scratch_shapes=[pltpu.VMEM((EXPERT_BLOCK, c), U32), pltpu.SemaphoreType.DMA(()),
                        pltpu.SemaphoreType.DMA(())],
        compiler_params=_cparams(("arbitrary",)), name="dispatch",
    )(dest_flat, zflag, h1p)


def _expert_kernel(blk_e_ref, nact_ref, next_e_ref, xs_ref, wg_hbm, wu_hbm, wd_hbm, ys_ref,
                   wg32, wu32, wd32, wgu_b, wd_b, sems, slot_ref):
    b = pl.program_id(0)
    e = blk_e_ref[b]
    de = wg32.shape[2]

    def weight_copies(expert, slot):
        return (pltpu.make_async_copy(wg_hbm.at[expert], wg32.at[slot], sems.at[slot, 0]),
                pltpu.make_async_copy(wu_hbm.at[expert], wu32.at[slot], sems.at[slot, 1]),
                pltpu.make_async_copy(wd_hbm.at[expert], wd32.at[slot], sems.at[slot, 2]))

    @pl.when(b == 0)
    def _():
        slot_ref[0] = 0
        for c in weight_copies(e, 0):
            c.start()

    active = b < nact_ref[0]
    first_of_expert = jnp.logical_or(b == 0, blk_e_ref[jnp.maximum(b - 1, 0)] != e)

    @pl.when(jnp.logical_and(active, first_of_expert))
    def _():
        slot = slot_ref[0]
        for c in weight_copies(e, slot):
            c.wait()
        wgu_b[:, 0:de] = wg32[slot].astype(BF16)
        wgu_b[:, de:2 * de] = wu32[slot].astype(BF16)
        wd_b[...] = wd32[slot].astype(BF16)
        nxt = next_e_ref[b]

        @pl.when(nxt >= 0)
        def _():
            for c in weight_copies(nxt, 1 - slot):
                c.start()

        slot_ref[0] = 1 - slot

    @pl.when(active)
    def _():
        x = _unpack_halves(xs_ref[...]).astype(BF16)
        h = _dot(x, wgu_b[...])
        hb = (_silu(h[:, 0:de]) * h[:, de:2 * de]).astype(BF16)
        ys_ref[...] = _pack_halves(_dot(hb, wd_b[...]))

    @pl.when(jnp.logical_not(active))
    def _():
        ys_ref[...] = jnp.zeros(ys_ref.shape, ys_ref.dtype)


def _experts(xs, w_gate, w_up, w_down, blk_e, n_active, next_e):
    n_exp, d, de = w_gate.shape
    nb = blk_e.shape[0]
    rows = nb * EXPERT_BLOCK

    def live(b, nact):
        return jnp.minimum(b, jnp.maximum(nact[0] - 1, 0))

    hbm = pl.BlockSpec(memory_space=pl.ANY)
    grid_spec = pltpu.PrefetchScalarGridSpec(
        num_scalar_prefetch=3, grid=(nb,),
        in_specs=[pl.BlockSpec((EXPERT_BLOCK, d // 2), lambda b, be, na, ne: (live(b, na), 0)), hbm, hbm, hbm],
        out_specs=pl.BlockSpec((EXPERT_BLOCK, d // 2), lambda b, be, na, ne: (b, 0)),
        scratch_shapes=[pltpu.VMEM((2, d, de), F32), pltpu.VMEM((2, d, de), F32), pltpu.VMEM((2, de, d), F32),
                        pltpu.VMEM((d, 2 * de), BF16), pltpu.VMEM((de, d), BF16),
                        pltpu.SemaphoreType.DMA((2, 3)), pltpu.SMEM((1,), I32)])
    return pl.pallas_call(
        _expert_kernel, grid_spec=grid_spec,
        out_shape=jax.ShapeDtypeStruct((rows, d // 2), U32),
        compiler_params=_cparams(("arbitrary",)), name="experts",
    )(blk_e, n_active, next_e, xs, w_gate, w_up, w_down)


def _combine_ln_kernel(src_ref, h1_ref, h1p_ref, gw_ref, ys_hbm, wg_ref, wu_ref, wd_ref, lg_ref, lb_ref,
                       yp_ref, ysm_ref, g_s, sem, *, ntp):
    i = pl.program_id(0)
    tm = h1_ref.shape[0]

    def issue(t, _):
        for k in range(TOP_K):
            r = src_ref[t * TOP_K + k]
            pltpu.make_async_copy(ys_hbm.at[pl.ds(r, 1), :], g_s.at[k, pl.ds(t, 1), :], sem).start()
        return 0

    lax.fori_loop(0, tm, issue, 0)
    xb = _unpack_halves(h1p_ref[...]).astype(BF16)
    hb = (_silu(_dot(xb, wg_ref[...])) * _dot(xb, wu_ref[...])).astype(BF16)
    f = _dot(hb, wd_ref[...])
    for k in range(TOP_K):
        pltpu.make_async_copy(ys_hbm.at[pl.ds(0, tm), :], g_s.at[k], sem).wait()
    gw = gw_ref[...]
    for k in range(TOP_K):
        f = f + gw[:, k:k + 1] * _unpack_halves(g_s[k])
    y = _layer_norm_rows(ALPHA * h1_ref[...] + f, lg_ref[...], lb_ref[...])

    @pl.when(i < ntp)
    def _():
        yp_ref[...] = y

    @pl.when(i == ntp)
    def _():
        ysm_ref[...] = y


def _combine_ln(src_flat, h1, h1p, gw, ys, ws_gate, ws_up, ws_down, ln_g, ln_b, *, tm):
    n_all, d = h1.shape
    ntp = n_all // tm - 1
    kern = functools.partial(_combine_ln_kernel, ntp=ntp)
    return pl.pallas_call(
        kern, grid=(ntp + 1,),
        in_specs=[pl.BlockSpec((tm * TOP_K,), lambda i: (i,), memory_space=pltpu.SMEM),
                  pl.BlockSpec((tm, d), lambda i: (i, 0)), pl.BlockSpec((tm, d // 2), lambda i: (i, 0)),
                  pl.BlockSpec((tm, TOP_K), lambda i: (i, 0)), pl.BlockSpec(memory_space=pl.ANY),
                  _const_spec(ws_gate.shape), _const_spec(ws_up.shape),
                  _const_spec(ws_down.shape), _const_spec(ln_g.shape), _const_spec(ln_b.shape)],
        out_specs=[pl.BlockSpec((tm, d), lambda i: (jnp.minimum(i, ntp - 1), 0)),
                   pl.BlockSpec((tm, d), lambda i: (0, 0))],
        out_shape=[jax.ShapeDtypeStruct((ntp * tm, d), F32), jax.ShapeDtypeStruct((tm, d), F32)],
        scratch_shapes=[pltpu.VMEM((TOP_K, tm, d // 2), U32), pltpu.SemaphoreType.DMA(())],
        compiler_params=_cparams(("arbitrary",)), name="combine_ln",
    )(src_flat, h1, h1p, gw, ys, ws_gate, ws_up, ws_down, ln_g, ln_b)


def _prep_weights(w_in, b_gate, conv_w, conv_b, norm_g, w_out, ln1_g, ln1_b, w_router, router_bias,
                  w_s_gate, w_s_up, w_s_down, ln2_g, ln2_b):
    d = w_in.shape[0]
    n_main = QK_COLS + 2 * VM_COLS + 3 * FOX_HEAD_COLS
    n_gate = w_in.shape[1] - n_main
    w_main = w_in[:, :n_main].astype(BF16)
    wg = jnp.zeros((d, LANES), F32).at[:, :n_gate].set(w_in[:, n_main:])
    bg = jnp.zeros((LANES,), F32).at[:n_gate].set(b_gate)
    for c in FOX_GATE_COPIES[1:]:
        wg = wg.at[:, c:c + HF].set(w_in[:, n_main + 2 * HM:])
        bg = bg.at[c:c + HF].set(b_gate[2 * HM:])
    wg = wg.astype(BF16)
    wr_hi = w_router.T.astype(BF16)
    wr_lo = (w_router.T - wr_hi.astype(F32)).astype(BF16)
    return dict(
        w_main=w_main, w_gate=wg, w_gate_t=wg.T, bg_row=bg[None, :], bg_col=bg[:, None],
        conv_w8=jnp.zeros((SUBLANES, QK_COLS), F32).at[:CONV_W].set(conv_w), conv_b=conv_b[None, :],
        norm_g=norm_g[None, :], w_out=w_out.astype(BF16), ln1_g=ln1_g[None, :], ln1_b=ln1_b[None, :],
        wr_hi=wr_hi, wr_lo=wr_lo, rbias=router_bias[:, None],
        ws_gate=w_s_gate.astype(BF16), ws_up=w_s_up.astype(BF16), ws_down=w_s_down.astype(BF16),
        ln2_g=ln2_g[None, :], ln2_b=ln2_b[None, :])


def _pad_rows(a, bsz, t, tp, value=0.0):
    w = a.shape[-1]
    a = a.reshape(bsz, t, w)
    pad = jnp.full((bsz, tp - t, w), value, a.dtype) if not hasattr(value, "shape") else \
        jnp.broadcast_to(value.astype(a.dtype), (bsz, tp - t, w))
    return jnp.concatenate([a, pad], axis=1).reshape(bsz * tp, w)


def _mixer_prompt(x, wts, *, tm, chunk, tb):
    bsz, t, d = x.shape
    n = bsz * t
    x2 = x.reshape(n, d)
    uqk, vm, om, qt, kf, vf, kb, vt, g, gt = _inproj(
        x2, wts["w_main"], wts["w_gate"], wts["w_gate_t"], wts["bg_row"], wts["bg_col"],
        tm=tm, seq_len=t, prompt=True)
    fcols, lf = _fox_cumsum(g, bsz, t, min(512, t))
    zeros = lambda *s: jnp.zeros(s, F32)
    hm, c1, n1, m1 = _mlstm(uqk, vm, om, g, gt, zeros(bsz, SUBLANES, QK_COLS), wts["conv_w8"], wts["conv_b"],
                            wts["norm_g"], zeros(bsz, HM, DQK, DV), zeros(bsz, HM, 1, DQK),
                            zeros(bsz, HM, 1, LANES), bsz=bsz, seq_len=t, chunk=chunk)
    of = _fox_prompt(qt, kb, vt, fcols, tb=tb).reshape(n, FOX_HEAD_COLS)
    state = (kf.reshape(bsz, t, HF, DHF), vf.reshape(bsz, t, HF, DHF),
             lf[:, 2 * HM:2 * HM + HF].reshape(bsz, t, HF), c1, n1[:, :, 0, :], m1[:, :, 0, 0],
             uqk.reshape(bsz, t, QK_COLS)[:, t - (CONV_W - 1):, :])
    return x2, hm, of, state


def _mixer_sample(x, conv_hist, c0, n0, m0, ck, cv, clogf, wts, *, chunk, kb):
    bsz, t, d = x.shape
    n = bsz * t
    past = ck.shape[1]
    x2 = x.reshape(n, d)
    uqk, vm, om, qf, kf, vf, g, gt = _inproj(
        x2, wts["w_main"], wts["w_gate"], wts["w_gate_t"], wts["bg_row"], wts["bg_col"],
        tm=n, seq_len=t, prompt=False)
    gpad = jnp.concatenate([jnp.full((HM,), NEG_BIG, F32), jnp.full((HM,), -NEG_BIG, F32),
                            jnp.zeros((LANES - 2 * HM,), F32)])
    g_p = _pad_rows(g, bsz, t, chunk, gpad)
    hist8 = jnp.concatenate([jnp.zeros((bsz, SUBLANES - (CONV_W - 1), QK_COLS), F32), conv_hist], axis=1)
    hm_p, c1, n1, m1 = _mlstm(
        _pad_rows(uqk, bsz, t, chunk), _pad_rows(vm, bsz, t, chunk), _pad_rows(om, bsz, t, chunk),
        g_p, g_p.T, hist8, wts["conv_w8"], wts["conv_b"], wts["norm_g"],
        c0, n0[:, :, None, :], jnp.broadcast_to(m0[:, :, None, None], (bsz, HM, 1, LANES)),
        bsz=bsz, seq_len=chunk, chunk=chunk)
    hm = hm_p.reshape(bsz, chunk, VM_COLS)[:, :t].reshape(n, VM_COLS)
    lf_new = _log_sigmoid_rows(g[:, 2 * HM:2 * HM + HF])
    q4 = qf.reshape(bsz, t, HF, DHF)
    eye = jnp.eye(HF, dtype=BF16)
    qbd = jnp.einsum("bthd,hg->bhdgt", q4, eye).reshape(bsz, FOX_HEAD_COLS, HF * t)
    rows_new = LANES
    expand = lambda a: jnp.repeat(a, t, axis=-1)
    pad3 = lambda a: jnp.concatenate(
        [a, jnp.zeros((bsz, rows_new - t, a.shape[-1]), a.dtype)], axis=1)
    of = _fox_sample(qbd, ck.reshape(bsz, past, FOX_HEAD_COLS), cv.reshape(bsz, past, FOX_HEAD_COLS),
                     expand(clogf), pad3(kf.reshape(bsz, t, FOX_HEAD_COLS)),
                     pad3(vf.reshape(bsz, t, FOX_HEAD_COLS)), pad3(expand(lf_new.reshape(bsz, t, HF))),
                     kb=kb, n_new=t).reshape(n, FOX_HEAD_COLS)
    state = (kf.reshape(bsz, t, HF, DHF), vf.reshape(bsz, t, HF, DHF), lf_new.reshape(bsz, t, HF),
             c1, n1[:, :, 0, :], m1[:, :, 0, 0],
             jnp.concatenate([conv_hist, uqk.reshape(bsz, t, QK_COLS)], axis=1)[:, -(CONV_W - 1):, :])
    return x2, hm, of, state


def _log_sigmoid_rows(a):
    n, w = a.shape
    ap = jnp.zeros((n, LANES), F32).at[:, :w].set(a)

    def kern(a_ref, o_ref):
        o_ref[...] = _log_sigmoid(a_ref[...])

    out = pl.pallas_call(kern, out_shape=jax.ShapeDtypeStruct((n, LANES), F32), name="log_sigmoid")(ap)
    return out[:, :w]


def _dispatch_tables(eidx, rank, counts, n_rows_total):
    blk = EXPERT_BLOCK
    pcounts = (counts + blk - 1) // blk * blk
    pend = jnp.cumsum(pcounts)
    pstart = pend - pcounts
    dest = pstart[eidx] + rank
    nb = n_rows_total // blk
    blk_e = jnp.minimum(jnp.searchsorted(pend, jnp.arange(nb, dtype=I32) * blk, side="right"),
                        N_EXPERTS - 1).astype(I32)
    n_active = (pend[-1] // blk).astype(I32)
    idx = jnp.arange(nb, dtype=I32)
    starts = jnp.concatenate([jnp.ones((1,), bool), blk_e[1:] != blk_e[:-1]])
    next_start = lax.cummin(jnp.where(starts, idx, nb), reverse=True)
    next_start = jnp.concatenate([next_start[1:], jnp.full((1,), nb, I32)])
    next_e = jnp.where(next_start < n_active, blk_e[jnp.minimum(next_start, nb - 1)], -1).astype(I32)
    last_of_run = jnp.concatenate([starts[1:], jnp.ones((1,), bool)])
    zflag = ((idx >= n_active - 1) | last_of_run).astype(I32)
    return dest.astype(I32), blk_e, n_active.reshape(1), next_e, zflag


def kernel(x_prompt, x_sample, cache_fox_k, cache_fox_v, cache_fox_logf, state_mlstm_C, state_mlstm_n,
           state_mlstm_m, state_conv, w_in, b_gate, conv_w, conv_b, mlstm_norm_g, w_out, ln1_g, ln1_b,
           w_router, router_bias, w_e_gate, w_e_up, w_e_down, w_s_gate, w_s_up, w_s_down, ln2_g, ln2_b):
    l = 0
    bp, tp, d = x_prompt.shape
    bs, ts, _ = x_sample.shape
    n_p, n_s = bp * tp, bs * ts
    n_tot = n_p + n_s
    wts = _prep_weights(w_in[l], b_gate[l], conv_w[l], conv_b[l], mlstm_norm_g[l], w_out[l], ln1_g[l], ln1_b[l],
                        w_router[l], router_bias[l], w_s_gate[l], w_s_up[l], w_s_down[l], ln2_g[l], ln2_b[l])

    tm_p = min(256, tp)
    xp2, hm_p, of_p, st_p = _mixer_prompt(x_prompt, wts, tm=tm_p, chunk=min(256, tp), tb=min(512, tp))
    xs2, hm_s, of_s, st_s = _mixer_sample(
        x_sample, state_conv[l], state_mlstm_C[l], state_mlstm_n[l], state_mlstm_m[l],
        cache_fox_k[l], cache_fox_v[l], cache_fox_logf[l], wts, chunk=LANES, kb=min(512, cache_fox_k.shape[2]))

    tm = tm_p
    padr = lambda a: jnp.concatenate([a, jnp.zeros((tm - n_s, a.shape[1]), a.dtype)], axis=0)
    h1, h1p, eidx_t, gw_t, rank_t, cnt = _outproj_router(
        hm_p, of_p, xp2, padr(hm_s), padr(of_s), padr(xs2),
        wts["w_out"], wts["ln1_g"], wts["ln1_b"], wts["wr_hi"], wts["wr_lo"], wts["rbias"],
        tm=tm, n_valid=n_tot)
    n_all = n_p + tm
    n_pad = n_all - n_tot
    eidx = eidx_t[:, :n_tot].T
    gw = gw_t[:, :n_tot].T
    rank = rank_t[:, :n_tot].T
    counts = cnt[:, 0].astype(I32)

    nk = n_tot * TOP_K
    n_blocks = min(N_EXPERTS, nk) + nk // EXPERT_BLOCK
    rows_total = n_blocks * EXPERT_BLOCK
    dest, blk_e, n_active, next_e, zflag = _dispatch_tables(eidx, rank, counts, rows_total)
    park = rows_total + jnp.arange(n_pad * TOP_K, dtype=I32).reshape(n_pad, TOP_K)
    xs = _dispatch(h1p, jnp.concatenate([dest, park], axis=0).reshape(-1), zflag,
                   tm=tm, rows_alloc=rows_total + n_pad * TOP_K)
    de = w_e_gate.shape[-1]
    ys = _experts(xs, w_e_gate.reshape(-1, d, de), w_e_up.reshape(-1, d, de), w_e_down.reshape(-1, de, d),
                  blk_e + l * N_EXPERTS, n_active, jnp.where(next_e >= 0, next_e + l * N_EXPERTS, -1))
    src = jnp.concatenate([dest, jnp.zeros((n_pad, TOP_K), I32)], axis=0)
    gw_all = jnp.concatenate([gw, jnp.zeros((n_pad, TOP_K), F32)], axis=0)
    y_p, y_s = _combine_ln(src.reshape(-1), h1, h1p, gw_all, ys, wts["ws_gate"], wts["ws_up"], wts["ws_down"],
                           wts["ln2_g"], wts["ln2_b"], tm=tm)
    y_p = y_p.reshape(bp, tp, d)
    y_s = y_s[:n_s].reshape(bs, ts, d)

    stack = lambda s: tuple(a[None] for a in s)
    return (y_p, y_s) + stack(st_p) + stack(st_s)
```

```python
import functools
import math

import jax
import jax.numpy as jnp
from jax import lax
from jax.experimental import pallas as pl
from jax.experimental.pallas import tpu as pltpu

F32 = jnp.float32
BF16 = jnp.bfloat16
I32 = jnp.int32
U32 = jnp.uint32

HM, DQK, DV = 4, 128, 256
HF, DHF = 8, 128
CONV_W = 4
QK_COLS = 2 * HM * DQK
VM_COLS = HM * DV
FOX_HEAD_COLS = HF * DHF
N_EXPERTS, TOP_K, N_GROUPS, TOP_GROUPS = 256, 8, 8, 4
GROUP_SIZE = N_EXPERTS // N_GROUPS
ROUTE_SCALE = 2.5
DEPTH = 1
ALPHA = (2 * DEPTH) ** 0.25
LN_EPS = 1e-5
HEAD_EPS = 1e-6
LOG2E = math.log2(math.e)

LANES = 128
SUBLANES = 8
VMEM_LIMIT_BYTES = 56 * 1024 * 1024

NEG_BIG = -1e30
EXPERT_BLOCK = 256


def _dot(a, b):
    return jnp.dot(a, b, preferred_element_type=F32)


def _dot_nt(a, b):
    return lax.dot_general(a, b, (((1,), (1,)), ((), ())), preferred_element_type=F32)


def _dot_tn(a, b):
    return lax.dot_general(a, b, (((0,), (0,)), ((), ())), preferred_element_type=F32)


def _split3(x):
    hi = x.astype(BF16)
    r1 = x - hi.astype(F32)
    mid = r1.astype(BF16)
    lo = (r1 - mid.astype(F32)).astype(BF16)
    return hi, mid, lo


def _split2(x):
    hi = x.astype(BF16)
    lo = (x - hi.astype(F32)).astype(BF16)
    return hi, lo


def _pack_halves(x):
    c = x.shape[1] // 2
    lo = lax.bitcast_convert_type(x[:, :c].astype(BF16).astype(F32), U32)
    hi = lax.bitcast_convert_type(x[:, c:].astype(BF16).astype(F32), U32)
    return (lo >> 16) | hi


def _unpack_halves(p):
    lo = lax.bitcast_convert_type(p << 16, F32)
    hi = lax.bitcast_convert_type(p & jnp.uint32(0xFFFF0000), F32)
    return jnp.concatenate([lo, hi], axis=1)


def _log_sigmoid(x):
    return jnp.minimum(x, 0.0) - jnp.log1p(jnp.exp(-jnp.abs(x)))


def _silu(x):
    return x * jax.nn.sigmoid(x)


def _cparams(sem):
    return pltpu.CompilerParams(dimension_semantics=sem, vmem_limit_bytes=VMEM_LIMIT_BYTES)


def _const_spec(shape):
    nd = len(shape)
    return pl.BlockSpec(shape, lambda *_: (0,) * nd, pipeline_mode=pl.Buffered(1))


def _inproj_kernel(x_ref, w_ref, wg_ref, wgt_ref, bg_ref, bgt_ref, *outs, prompt, q_scale):
    xb = x_ref[...].astype(BF16)
    if prompt:
        uqk_ref, vm_ref, om_ref, qf_ref, kf_ref, vf_ref, kb_ref, vt_ref, g_ref, gt_ref = outs
    else:
        uqk_ref, vm_ref, om_ref, qf_ref, kf_ref, vf_ref, g_ref, gt_ref = outs
    o = 0
    uqk_ref[...] = _dot(xb, w_ref[:, o:o + QK_COLS])
    o += QK_COLS
    vm_ref[...] = _dot(xb, w_ref[:, o:o + VM_COLS]).astype(BF16)
    o += VM_COLS
    om_ref[...] = jax.nn.sigmoid(_dot(xb, w_ref[:, o:o + VM_COLS])).astype(BF16)
    o += VM_COLS
    q = _dot(xb, w_ref[:, o:o + FOX_HEAD_COLS]) * q_scale
    o += FOX_HEAD_COLS
    k = _dot(xb, w_ref[:, o:o + FOX_HEAD_COLS])
    o += FOX_HEAD_COLS
    v = _dot(xb, w_ref[:, o:o + FOX_HEAD_COLS])
    kf_ref[...] = k
    vf_ref[...] = v
    if prompt:
        qf_ref[0] = q.T.astype(BF16)
        vt_ref[0] = v.T.astype(BF16)
        for h in range(HF):
            kb_ref[0, h] = k[:, h * DHF:(h + 1) * DHF].astype(BF16)
    else:
        qf_ref[...] = q.astype(BF16)
    g_ref[...] = _dot(xb, wg_ref[...]) + bg_ref[...]
    gt_ref[...] = _dot_nt(wgt_ref[...], xb) + bgt_ref[...]


def _inproj(x2, w_main, w_gate, w_gate_t, b_gate_row, b_gate_col, *, tm, seq_len, prompt):
    n, d = x2.shape
    nt = n // tm
    grid = (nt,)
    row = lambda w: pl.BlockSpec((tm, w), lambda i: (i, 0))
    out_shape = [jax.ShapeDtypeStruct((n, QK_COLS), F32),
                 jax.ShapeDtypeStruct((n, VM_COLS), BF16),
                 jax.ShapeDtypeStruct((n, VM_COLS), BF16)]
    out_specs = [row(QK_COLS), row(VM_COLS), row(VM_COLS)]
    f32_rows = jax.ShapeDtypeStruct((n, FOX_HEAD_COLS), F32)
    if prompt:
        bsz = n // seq_len
        tpb = seq_len // tm
        t_shape = jax.ShapeDtypeStruct((bsz, FOX_HEAD_COLS, seq_len), BF16)
        t_spec = pl.BlockSpec((1, FOX_HEAD_COLS, tm), lambda i: (i // tpb, 0, i % tpb))
        out_shape += [t_shape, f32_rows, f32_rows,
                      jax.ShapeDtypeStruct((bsz, HF, seq_len, DHF), BF16), t_shape]
        out_specs += [t_spec, row(FOX_HEAD_COLS), row(FOX_HEAD_COLS),
                      pl.BlockSpec((1, HF, tm, DHF), lambda i: (i // tpb, 0, i % tpb, 0)), t_spec]
    else:
        out_shape += [jax.ShapeDtypeStruct((n, FOX_HEAD_COLS), BF16), f32_rows, f32_rows]
        out_specs += [row(FOX_HEAD_COLS), row(FOX_HEAD_COLS), row(FOX_HEAD_COLS)]
    out_shape += [jax.ShapeDtypeStruct((n, LANES), F32), jax.ShapeDtypeStruct((LANES, n), F32)]
    out_specs += [row(LANES), pl.BlockSpec((LANES, tm), lambda i: (0, i))]
    kern = functools.partial(_inproj_kernel, prompt=prompt, q_scale=DHF ** -0.5 * LOG2E)
    return pl.pallas_call(
        kern, grid=grid,
        in_specs=[row(d), _const_spec(w_main.shape), _const_spec(w_gate.shape),
                  _const_spec(w_gate_t.shape), _const_spec(b_gate_row.shape), _const_spec(b_gate_col.shape)],
        out_specs=out_specs, out_shape=out_shape,
        compiler_params=_cparams(("arbitrary",)), name="inproj",
    )(x2, w_main, w_gate, w_gate_t, b_gate_row, b_gate_col)


def _fox_cumsum_kernel(g_ref, fc_ref, lf_ref, carry_s):
    @pl.when(pl.program_id(1) == 0)
    def _():
        carry_s[...] = jnp.zeros(carry_s.shape, F32)

    lf = _log_sigmoid(g_ref[...])
    lf_ref[...] = lf
    r = lf.shape[0]
    tril = (lax.broadcasted_iota(I32, (r, r), 1) <= lax.broadcasted_iota(I32, (r, r), 0)).astype(BF16)
    cs = carry_s[...] + sum(_dot(tril, p) for p in _split3(lf))
    carry_s[...] = cs[r - 1:r, :]
    fc_ref[...] = cs * LOG2E


def _fox_cumsum(g, bsz, seq_len, rows):
    nc = seq_len // rows
    spec = pl.BlockSpec((rows, LANES), lambda b, c: (b * nc + c, 0))
    shape = jax.ShapeDtypeStruct((bsz * seq_len, LANES), F32)
    return pl.pallas_call(
        _fox_cumsum_kernel, grid=(bsz, nc),
        in_specs=[spec], out_specs=[spec, spec], out_shape=[shape, shape],
        scratch_shapes=[pltpu.VMEM((1, LANES), F32)],
        compiler_params=_cparams(("arbitrary", "arbitrary")), name="fox_cumsum",
    )(g)


def _mlstm_kernel(uqk_ref, v_ref, om_ref, g_ref, gt_ref, hist_ref, cw_ref, cb_ref, ng_ref,
                  c0_ref, n0_ref, m0_ref, h_ref, c1_ref, n1_ref, m1_ref,
                  c_s, n_s, m_s, prev_s):
    c_idx = pl.program_id(1)
    nc = pl.num_programs(1)
    L = uqk_ref.shape[0]

    @pl.when(c_idx == 0)
    def _():
        c_s[...] = c0_ref[0]
        n_s[...] = n0_ref[0]
        m_s[...] = m0_ref[0]
        prev_s[...] = hist_ref[0]

    z = uqk_ref[...]
    ext = jnp.concatenate([prev_s[...], z], axis=0)
    prev_s[...] = z[L - SUBLANES:, :]
    cw = cw_ref[...]
    acc = cb_ref[...] + ext[SUBLANES:SUBLANES + L] * cw[CONV_W - 1:CONV_W]
    for j in range(CONV_W - 1):
        off = SUBLANES - (CONV_W - 1) + j
        acc = acc + ext[off:off + L] * cw[j:j + 1]
    qk = _silu(acc)

    g = g_ref[...]
    gt = gt_ref[...]
    lsg = _log_sigmoid(g)
    lsgt = _log_sigmoid(gt)
    ri = lax.broadcasted_iota(I32, (L, L), 0)
    ci = lax.broadcasted_iota(I32, (L, L), 1)
    causal = ci <= ri
    tril = causal.astype(BF16)
    triu = (ri <= ci).astype(BF16)
    b_cols = sum(_dot(tril, p) for p in _split3(lsg))
    b_rows = sum(_dot(p, triu) for p in _split3(lsgt))

    for h in range(HM):
        q = qk[:, h * DQK:(h + 1) * DQK]
        k = qk[:, HM * DQK + h * DQK:HM * DQK + (h + 1) * DQK] * (DQK ** -0.5)
        qb = q.astype(BF16)
        kb = k.astype(BF16)
        v = v_ref[:, h * DV:(h + 1) * DV]
        b_col = b_cols[:, HM + h:HM + h + 1]
        b_row = b_rows[HM + h:HM + h + 1, :]
        li_col = g[:, h:h + 1]
        li_row = gt[h:h + 1, :]
        m_prev = m_s[h][:, 0:1]
        c_prev = c_s[h]
        n_prev = n_s[h]

        c_row = li_row - b_row
        dmat = jnp.where(causal, b_col + c_row, -jnp.inf)
        inter = b_col + m_prev
        m_t = jnp.maximum(inter, jnp.max(dmat, axis=1, keepdims=True))
        w_intra = jnp.exp(dmat - m_t)
        w_state = jnp.exp(inter - m_t)
        s = _dot_nt(qb, kb) * w_intra
        num = _dot(s.astype(BF16), v) + w_state * _dot(qb, c_prev.astype(BF16))
        den = jnp.sum(s, axis=1, keepdims=True) + w_state * jnp.sum(q * n_prev, axis=1, keepdims=True)
        hh = num / jnp.maximum(jnp.abs(den), jnp.exp(-m_t))

        b_end = b_col[L - 1:L, :]
        a_col = b_end - b_col + li_col
        a_row = b_end + c_row
        m_new = jnp.maximum(b_end + m_prev, jnp.max(a_row, axis=1, keepdims=True))
        decay = jnp.exp(b_end + m_prev - m_new)
        wk = jnp.exp(a_col - m_new) * k
        c_s[h] = decay * c_prev + _dot_tn(wk.astype(BF16), v)
        n_s[h] = decay * n_prev + jnp.sum(wk, axis=0, keepdims=True)
        m_s[h] = jnp.broadcast_to(m_new, (1, LANES))

        mu = jnp.mean(hh, axis=1, keepdims=True)
        hc = hh - mu
        var = jnp.mean(hc * hc, axis=1, keepdims=True)
        hn = hc * lax.rsqrt(var + HEAD_EPS)
        sl = slice(h * DV, (h + 1) * DV)
        h_ref[:, sl] = (om_ref[:, sl].astype(F32) * hn * ng_ref[:, sl]).astype(BF16)

    @pl.when(c_idx == nc - 1)
    def _():
        c1_ref[0] = c_s[...]
        n1_ref[0] = n_s[...]
        m1_ref[0] = m_s[...]


def _mlstm(uqk, vm, om, g, g_t, hist8, conv_w8, conv_b, norm_g, c0, n0, m0, *, bsz, seq_len, chunk):
    nc = seq_len // chunk
    rowspec = lambda w: pl.BlockSpec((chunk, w), lambda b, c: (b * nc + c, 0))
    st = lambda shape: pl.BlockSpec((1,) + shape, lambda b, c: (b,) + (0,) * len(shape))
    return pl.pallas_call(
        _mlstm_kernel, grid=(bsz, nc),
        in_specs=[rowspec(QK_COLS), rowspec(VM_COLS), rowspec(VM_COLS), rowspec(LANES),
                  pl.BlockSpec((SUBLANES, chunk), lambda b, c: (0, b * nc + c)),
                  st((SUBLANES, QK_COLS)),
                  pl.BlockSpec((SUBLANES, QK_COLS), lambda b, c: (0, 0)),
                  pl.BlockSpec((1, QK_COLS), lambda b, c: (0, 0)),
                  pl.BlockSpec((1, VM_COLS), lambda b, c: (0, 0)),
                  st((HM, DQK, DV)), st((HM, 1, DQK)), st((HM, 1, LANES))],
        out_specs=[rowspec(VM_COLS), st((HM, DQK, DV)), st((HM, 1, DQK)), st((HM, 1, LANES))],
        out_shape=[jax.ShapeDtypeStruct((bsz * seq_len, VM_COLS), BF16),
                   jax.ShapeDtypeStruct((bsz, HM, DQK, DV), F32),
                   jax.ShapeDtypeStruct((bsz, HM, 1, DQK), F32),
                   jax.ShapeDtypeStruct((bsz, HM, 1, LANES), F32)],
        scratch_shapes=[pltpu.VMEM((HM, DQK, DV), F32), pltpu.VMEM((HM, 1, DQK), F32),
                        pltpu.VMEM((HM, 1, LANES), F32), pltpu.VMEM((SUBLANES, QK_COLS), F32)],
        compiler_params=_cparams(("arbitrary", "arbitrary")), name="mlstm",
    )(uqk, vm, om, g, g_t, hist8, conv_w8, conv_b, norm_g, c0, n0, m0)


FOX_GATE_COPIES = (8, 24, 40)


def _fox_prompt_kernel(qt_ref, k_ref, vt_ref, fc_ref, o_ref, kaug_s, u_s, p_s, acc_s, st_s, *, tb, prep_rows):
    h = pl.program_id(1)
    i = pl.program_id(2)
    seq = k_ref.shape[2]

    @pl.when(i == 0)
    def _():
        lane = lax.broadcasted_iota(I32, (prep_rows, LANES), 1)
        zero = jnp.zeros((prep_rows, LANES), BF16)

        def prep(c, _):
            rs = pl.ds(pl.multiple_of(c * prep_rows, prep_rows), prep_rows)
            hi, mid, lo = _split3(-fc_ref[rs, :])
            aug = jnp.where(lane == FOX_GATE_COPIES[0] + h, hi,
                            jnp.where(lane == FOX_GATE_COPIES[1] + h, mid,
                                      jnp.where(lane == FOX_GATE_COPIES[2] + h, lo, zero)))
            kaug_s[rs, 0:DHF] = k_ref[0, 0, rs, :]
            kaug_s[rs, DHF:2 * DHF] = aug
            return 0

        lax.fori_loop(0, seq // prep_rows, prep, 0)

    row = lax.broadcasted_iota(I32, (DHF, tb), 0)
    ones = jnp.where(row == FOX_GATE_COPIES[0] + h, 1.0,
                     jnp.where(row == FOX_GATE_COPIES[1] + h, 1.0,
                               jnp.where(row == FOX_GATE_COPIES[2] + h, 1.0, 0.0))).astype(BF16)
    qaug = jnp.concatenate([qt_ref[0], ones], axis=0)

    def key_rows(r):
        return pl.ds(pl.multiple_of(jnp.maximum(i - r, 0) * tb, tb), tb)

    def scores(r, slot):
        u_s[slot] = _dot(kaug_s[key_rows(r), :], qaug)

    def softmax(slot, diagonal):
        for c in range(tb // LANES):
            cs = slice(c * LANES, (c + 1) * LANES)
            u = u_s[slot, :, cs]
            if diagonal:
                kpos = lax.broadcasted_iota(I32, (tb, LANES), 0)
                qpos = lax.broadcasted_iota(I32, (tb, LANES), 1) + c * LANES
                u = jnp.where(kpos <= qpos, u, -jnp.inf)
            m = st_s[0:1, cs]
            m_new = jnp.maximum(m, jnp.max(u, axis=0, keepdims=True))
            alpha = jnp.exp2(m - m_new)
            p = jnp.exp2(u - m_new)
            st_s[0:1, cs] = m_new
            st_s[1:2, cs] = alpha * st_s[1:2, cs] + jnp.sum(p, axis=0, keepdims=True)
            st_s[2:3, cs] = alpha
            p_s[slot, :, cs] = p.astype(BF16)

    def accumulate(slot, r):
        acc_s[...] = st_s[2:3, :] * acc_s[...] + _dot(vt_ref[0, :, key_rows(r)], p_s[slot])

    def step(r, slot, diagonal=False, first=False, last=False):
        if not last:
            scores(r + 1, 1 - slot)
        if not first:
            accumulate(1 - slot, r - 1)
        softmax(slot, diagonal)

    def pair(s, carry):
        step(2 * s + 1, 1)
        step(2 * s + 2, 0)
        return carry

    acc_s[...] = jnp.zeros(acc_s.shape, F32)
    st_s[...] = jnp.concatenate([jnp.full((1, tb), -jnp.inf, F32), jnp.zeros((SUBLANES - 1, tb), F32)], axis=0)
    scores(0, 0)
    step(0, 0, diagonal=True, first=True)
    lax.fori_loop(0, i // 2, pair, 0)

    @pl.when(i % 2 == 1)
    def _():
        step(i, 1, last=True)
        accumulate(1, i)

    @pl.when(i % 2 == 0)
    def _():
        accumulate(0, i)

    o_ref[0] = (acc_s[...] / st_s[1:2, :]).T.astype(o_ref.dtype)


def _fox_prompt(q_t, k_hm, v_t, fcols, *, tb):
    bsz, _, seq_len, _ = k_hm.shape
    kern = functools.partial(_fox_prompt_kernel, tb=tb, prep_rows=min(512, seq_len))
    return pl.pallas_call(
        kern, grid=(bsz, HF, seq_len // tb),
        in_specs=[pl.BlockSpec((1, DHF, tb), lambda b, h, i: (b, h, i)),
                  pl.BlockSpec((1, 1, seq_len, DHF), lambda b, h, i: (b, h, 0, 0)),
                  pl.BlockSpec((1, DHF, seq_len), lambda b, h, i: (b, h, 0)),
                  pl.BlockSpec((seq_len, LANES), lambda b, h, i: (b, 0))],
        out_specs=pl.BlockSpec((1, tb, DHF), lambda b, h, i: (b, i, h)),
        out_shape=jax.ShapeDtypeStruct((bsz, seq_len, FOX_HEAD_COLS), BF16),
        scratch_shapes=[pltpu.VMEM((seq_len, 2 * DHF), BF16), pltpu.VMEM((2, tb, tb), F32),
                        pltpu.VMEM((2, tb, tb), BF16), pltpu.VMEM((DHF, tb), F32),
                        pltpu.VMEM((SUBLANES, tb), F32)],
        compiler_params=_cparams(("arbitrary", "arbitrary", "arbitrary")), name="fox_prompt",
    )(q_t, k_hm, v_t, fcols)


def _fox_sample_kernel(qbd_ref, ck_ref, cv_ref, clf_ref, kn_ref, vn_ref, lfn_ref, o_ref,
                       m_s, l_s, acc_s, f_s, *, kb, n_new):
    j = pl.program_id(1)
    nj = pl.num_programs(1)

    @pl.when(j == 0)
    def _():
        m_s[...] = jnp.full(m_s.shape, -jnp.inf, F32)
        l_s[...] = jnp.zeros(l_s.shape, F32)
        acc_s[...] = jnp.zeros(acc_s.shape, F32)
        f_s[...] = jnp.zeros(f_s.shape, F32)

    qbd = qbd_ref[0]
    eye = (lax.broadcasted_iota(I32, (LANES, LANES), 0) == lax.broadcasted_iota(I32, (LANES, LANES), 1))

    def to_col(row):
        return jnp.sum(jnp.where(eye, row, 0.0), axis=1, keepdims=True)

    def step(kf32, vf32, lf, valid):
        rows = kf32.shape[0]
        ri = lax.broadcasted_iota(I32, (rows, rows), 0)
        ci = lax.broadcasted_iota(I32, (rows, rows), 1)
        tril = (ci <= ri).astype(BF16)
        fk = f_s[...] + sum(_dot(tril, p) for p in _split3(lf))
        f_s[...] = fk[rows - 1:rows, :]
        u = _dot(kf32.astype(BF16), qbd) - fk * LOG2E
        if valid is not None:
            u = jnp.where(valid, u, -jnp.inf)
        m = m_s[...]
        m_new = jnp.maximum(m, jnp.max(u, axis=0, keepdims=True))
        alpha = jnp.exp2(m - m_new)
        p = jnp.exp2(u - m_new)
        l_s[...] = alpha * l_s[...] + jnp.sum(p, axis=0, keepdims=True)
        m_s[...] = m_new
        acc_s[...] = to_col(alpha) * acc_s[...] + _dot_tn(p.astype(BF16), vf32.astype(BF16))

    step(ck_ref[0], cv_ref[0], clf_ref[0], None)

    @pl.when(j == nj - 1)
    def _():
        rows = kn_ref.shape[1]
        r = lax.broadcasted_iota(I32, (rows, LANES), 0)
        t = lax.broadcasted_iota(I32, (rows, LANES), 1) % n_new
        step(kn_ref[0], vn_ref[0], lfn_ref[0], r <= t)
        inv = 1.0 / to_col(l_s[...])
        acc = acc_s[...] * inv
        for h in range(HF):
            o_ref[0, :, h * DHF:(h + 1) * DHF] = acc[h * n_new:(h + 1) * n_new,
                                                     h * DHF:(h + 1) * DHF].astype(o_ref.dtype)


def _fox_sample(qbd, ck, cv, clf_exp, k_new, v_new, lf_new_exp, *, kb, n_new, cache_off):
    bsz = qbd.shape[0]
    past = ck.shape[1]
    rows_new = k_new.shape[1]
    kern = functools.partial(_fox_sample_kernel, kb=kb, n_new=n_new)
    blk = lambda w: pl.BlockSpec((1, kb, w), lambda b, j: (b, j, 0))
    cblk = lambda w: pl.BlockSpec((1, kb, w), lambda b, j: (b + cache_off, j, 0))
    new = lambda w: pl.BlockSpec((1, rows_new, w), lambda b, j: (b, 0, 0))
    return pl.pallas_call(
        kern, grid=(bsz, past // kb),
        in_specs=[pl.BlockSpec((1, FOX_HEAD_COLS, LANES), lambda b, j: (b, 0, 0)),
                  cblk(FOX_HEAD_COLS), cblk(FOX_HEAD_COLS), blk(LANES),
                  new(FOX_HEAD_COLS), new(FOX_HEAD_COLS), new(LANES)],
        out_specs=pl.BlockSpec((1, n_new, FOX_HEAD_COLS), lambda b, j: (b, 0, 0)),
        out_shape=jax.ShapeDtypeStruct((bsz, n_new, FOX_HEAD_COLS), BF16),
        scratch_shapes=[pltpu.VMEM((1, LANES), F32), pltpu.VMEM((1, LANES), F32),
                        pltpu.VMEM((LANES, FOX_HEAD_COLS), F32), pltpu.VMEM((1, LANES), F32)],
        compiler_params=_cparams(("arbitrary", "arbitrary")), name="fox_sample",
    )(qbd, ck, cv, clf_exp, k_new, v_new, lf_new_exp)


def _layer_norm_rows(r, g, b):
    mu = jnp.mean(r, axis=1, keepdims=True)
    rc = r - mu
    var = jnp.mean(rc * rc, axis=1, keepdims=True)
    return rc * lax.rsqrt(var + LN_EPS) * g + b


def _outproj_router_kernel(hmp_ref, ofp_ref, xp_ref, hms_ref, ofs_ref, xs_ref,
                           wo_ref, lg_ref, lb_ref, wrh_ref, wrl_ref, rb_ref,
                           h1_ref, h1p_ref, eidx_ref, gw_ref, rank_ref, cnt_ref, cnt_s, *, ntp, n_valid):
    i = pl.program_id(0)
    tm = xp_ref.shape[0]

    @pl.when(i == 0)
    def _():
        cnt_s[...] = jnp.zeros(cnt_s.shape, F32)

    is_p = i < ntp
    hm = jnp.where(is_p, hmp_ref[...], hms_ref[...])
    of = jnp.where(is_p, ofp_ref[...], ofs_ref[...])
    x = jnp.where(is_p, xp_ref[...], xs_ref[...])
    half = hm.shape[1]
    mix = _dot(hm, wo_ref[0:half, :]) + _dot(of, wo_ref[half:, :])
    h1 = _layer_norm_rows(ALPHA * x + mix, lg_ref[...], lb_ref[...])
    h1_ref[...] = h1
    h1p_ref[...] = _pack_halves(h1)

    hh, hl = _split2(h1)
    logits = _dot_nt(wrh_ref[...], hh) + _dot_nt(wrl_ref[...], hh) + _dot_nt(wrh_ref[...], hl)
    scores = jax.nn.sigmoid(logits)
    biased = scores + rb_ref[...]

    neg = -jnp.inf
    sub = lax.broadcasted_iota(I32, (GROUP_SIZE, tm), 0).astype(F32)
    gs_rows = []
    for gi in range(N_GROUPS):
        grp = biased[gi * GROUP_SIZE:(gi + 1) * GROUP_SIZE, :]
        m1 = jnp.max(grp, axis=0, keepdims=True)
        i1 = jnp.min(jnp.where(grp == m1, sub, float(GROUP_SIZE)), axis=0, keepdims=True)
        m2 = jnp.max(jnp.where(sub == i1, neg, grp), axis=0, keepdims=True)
        gs_rows.append(m1 + m2)
    gsc = jnp.concatenate(gs_rows, axis=0)
    gid = lax.broadcasted_iota(I32, (N_GROUPS, tm), 0).astype(F32)
    gsel = jnp.zeros((N_GROUPS, tm), F32)
    for _ in range(TOP_GROUPS):
        m = jnp.max(gsc, axis=0, keepdims=True)
        idx = jnp.min(jnp.where(gsc == m, gid, float(N_GROUPS)), axis=0, keepdims=True)
        hit = gid == idx
        gsel = jnp.where(hit, 1.0, gsel)
        gsc = jnp.where(hit, neg, gsc)
    emask = jnp.concatenate(
        [jnp.broadcast_to(gsel[gi:gi + 1, :], (GROUP_SIZE, tm)) for gi in range(N_GROUPS)], axis=0)
    masked = jnp.where(emask > 0.0, biased, neg)

    eid = lax.broadcasted_iota(I32, (N_EXPERTS, tm), 0).astype(F32)
    sel = jnp.zeros((N_EXPERTS, tm), F32)
    idx_rows, gw_rows = [], []
    for _ in range(TOP_K):
        m = jnp.max(masked, axis=0, keepdims=True)
        idx = jnp.min(jnp.where(masked == m, eid, float(N_EXPERTS)), axis=0, keepdims=True)
        hit = eid == idx
        gw_rows.append(jnp.sum(jnp.where(hit, scores, 0.0), axis=0, keepdims=True))
        masked = jnp.where(hit, neg, masked)
        sel = jnp.where(hit, 1.0, sel)
        idx_rows.append(idx)
    tok = i * tm + lax.broadcasted_iota(I32, (1, tm), 1)
    sel = jnp.where(tok < n_valid, sel, 0.0)
    gsum = gw_rows[0]
    for r in gw_rows[1:]:
        gsum = gsum + r
    gscale = ROUTE_SCALE / gsum

    ri = lax.broadcasted_iota(I32, (tm, tm), 0)
    ci = lax.broadcasted_iota(I32, (tm, tm), 1)
    before = (ri < ci).astype(BF16)
    rankmat = _dot(sel.astype(BF16), before) + cnt_s[:, 0:1]
    cnt_s[...] = cnt_s[...] + jnp.sum(sel, axis=1, keepdims=True)
    for kk in range(TOP_K):
        hit = eid == idx_rows[kk]
        rk = jnp.sum(jnp.where(hit, rankmat, 0.0), axis=0, keepdims=True)
        eidx_ref[kk:kk + 1, :] = idx_rows[kk].astype(I32)
        gw_ref[kk:kk + 1, :] = gw_rows[kk] * gscale
        rank_ref[kk:kk + 1, :] = rk.astype(I32)

    @pl.when(i == pl.num_programs(0) - 1)
    def _():
        cnt_ref[...] = cnt_s[...]


def _outproj_router(hm_p, of_p, x_p, hm_s, of_s, x_s, w_out, ln_g, ln_b, wr_hi, wr_lo, rbias_col,
                    *, tm, n_valid):
    n_p, d = x_p.shape
    half = hm_p.shape[1]
    ntp = n_p // tm
    n_all = n_p + tm
    prow = lambda w: pl.BlockSpec((tm, w), lambda i: (jnp.minimum(i, ntp - 1), 0))
    srow = lambda w: pl.BlockSpec((tm, w), lambda i: (0, 0))
    orow = lambda w: pl.BlockSpec((tm, w), lambda i: (i, 0))
    trow = pl.BlockSpec((TOP_K, tm), lambda i: (0, i))
    cnt_spec = pl.BlockSpec((N_EXPERTS, LANES), lambda i: (0, 0))
    kern = functools.partial(_outproj_router_kernel, ntp=ntp, n_valid=n_valid)
    return pl.pallas_call(
        kern, grid=(ntp + 1,),
        in_specs=[prow(half), prow(half), prow(d), srow(half), srow(half), srow(d),
                  _const_spec(w_out.shape), _const_spec(ln_g.shape), _const_spec(ln_b.shape),
                  _const_spec(wr_hi.shape), _const_spec(wr_lo.shape), _const_spec(rbias_col.shape)],
        out_specs=[orow(d), orow(d // 2), trow, trow, trow, cnt_spec],
        out_shape=[jax.ShapeDtypeStruct((n_all, d), F32), jax.ShapeDtypeStruct((n_all, d // 2), U32),
                   jax.ShapeDtypeStruct((TOP_K, n_all), I32), jax.ShapeDtypeStruct((TOP_K, n_all), F32),
                   jax.ShapeDtypeStruct((TOP_K, n_all), I32), jax.ShapeDtypeStruct((N_EXPERTS, LANES), F32)],
        scratch_shapes=[pltpu.VMEM((N_EXPERTS, LANES), F32)],
        compiler_params=_cparams(("arbitrary",)), name="outproj_router",
    )(hm_p, of_p, x_p, hm_s, of_s, x_s, w_out, ln_g, ln_b, wr_hi, wr_lo, rbias_col)


def _dispatch_kernel(dest_ref, zflag_ref, h_ref, xs_hbm, zeros_s, sem, zsem, *, tm):
    i = pl.program_id(0)
    nblk = zflag_ref.shape[0]

    def zero_copy(b):
        rows = pl.ds(pl.multiple_of(b * EXPERT_BLOCK, EXPERT_BLOCK), EXPERT_BLOCK)
        return pltpu.make_async_copy(zeros_s, xs_hbm.at[rows, :], zsem)

    @pl.when(i == 0)
    def _():
        zeros_s[...] = jnp.zeros(zeros_s.shape, zeros_s.dtype)

        def start(b, _):
            @pl.when(zflag_ref[b] != 0)
            def _():
                zero_copy(b).start()
            return 0

        def wait(b, _):
            @pl.when(zflag_ref[b] != 0)
            def _():
                zero_copy(b).wait()
            return 0

        lax.fori_loop(0, nblk, start, 0)
        lax.fori_loop(0, nblk, wait, 0)

    def issue(t, _):
        for k in range(TOP_K):
            d = dest_ref[t * TOP_K + k]
            pltpu.make_async_copy(h_ref.at[pl.ds(t, 1), :], xs_hbm.at[pl.ds(d, 1), :], sem).start(priority=k % 2)
        return 0

    lax.fori_loop(0, tm, issue, 0)
    for _ in range(TOP_K):
        pltpu.make_async_copy(h_ref, xs_hbm.at[pl.ds(0, tm), :], sem).wait()


def _dispatch(h1p, dest_flat, zflag, *, tm, rows_alloc):
    n_all, c = h1p.shape
    kern = functools.partial(_dispatch_kernel, tm=tm)
    return pl.pallas_call(
        kern, grid=(n_all // tm,),
        in_specs=[pl.BlockSpec((tm * TOP_K,), lambda i: (i,), memory_space=pltpu.SMEM),
                  pl.BlockSpec(memory_space=pltpu.SMEM),
                  pl.BlockSpec((tm, c), lambda i: (i, 0))],
        out_specs=pl.BlockSpec(memory_space=pl.ANY),
        out_shape=jax.ShapeDtypeStruct((rows_alloc, c), U32),
        scratch_shapes=[pltpu.VMEM((EXPERT_BLOCK, c), U32), pltpu.SemaphoreType.DMA(()),
                        pltpu.SemaphoreType.DMA(())],
        compiler_params=_cparams(("arbitrary",)), name="dispatch",
    )(dest_flat, zflag, h1p)


def _expert_kernel(blk_e_ref, nact_ref, next_e_ref, xs_ref, wg_hbm, wu_hbm, wd_hbm, ys_ref,
                   wg32, wu32, wd32, wgu_b, wd_b, sems, slot_ref):
    b = pl.program_id(0)
    e = blk_e_ref[b]
    de = wg32.shape[2]

    def weight_copies(expert, slot):
        return (pltpu.make_async_copy(wg_hbm.at[expert], wg32.at[slot], sems.at[slot, 0]),
                pltpu.make_async_copy(wu_hbm.at[expert], wu32.at[slot], sems.at[slot, 1]),
                pltpu.make_async_copy(wd_hbm.at[expert], wd32.at[slot], sems.at[slot, 2]))

    @pl.when(b == 0)
    def _():
        slot_ref[0] = 0
        for c in weight_copies(e, 0):
            c.start()

    active = b < nact_ref[0]
    first_of_expert = jnp.logical_or(b == 0, blk_e_ref[jnp.maximum(b - 1, 0)] != e)

    @pl.when(jnp.logical_and(active, first_of_expert))
    def _():
        slot = slot_ref[0]
        for c in weight_copies(e, slot):
            c.wait()
        wgu_b[:, 0:de] = wg32[slot].astype(BF16)
        wgu_b[:, de:2 * de] = wu32[slot].astype(BF16)
        wd_b[...] = wd32[slot].astype(BF16)
        nxt = next_e_ref[b]

        @pl.when(nxt >= 0)
        def _():
            for c in weight_copies(nxt, 1 - slot):
                c.start()

        slot_ref[0] = 1 - slot

    @pl.when(active)
    def _():
        x = _unpack_halves(xs_ref[...]).astype(BF16)
        h = _dot(x, wgu_b[...])
        hb = (_silu(h[:, 0:de]) * h[:, de:2 * de]).astype(BF16)
        ys_ref[...] = _pack_halves(_dot(hb, wd_b[...]))

    @pl.when(jnp.logical_not(active))
    def _():
        ys_ref[...] = jnp.zeros(ys_ref.shape, ys_ref.dtype)


def _experts(xs, w_gate, w_up, w_down, blk_e, n_active, next_e):
    n_exp, d, de = w_gate.shape
    nb = blk_e.shape[0]
    rows = nb * EXPERT_BLOCK

    def live(b, nact):
        return jnp.minimum(b, jnp.maximum(nact[0] - 1, 0))

    hbm = pl.BlockSpec(memory_space=pl.ANY)
    grid_spec = pltpu.PrefetchScalarGridSpec(
        num_scalar_prefetch=3, grid=(nb,),
        in_specs=[pl.BlockSpec((EXPERT_BLOCK, d // 2), lambda b, be, na, ne: (live(b, na), 0)), hbm, hbm, hbm],
        out_specs=pl.BlockSpec((EXPERT_BLOCK, d // 2), lambda b, be, na, ne: (b, 0)),
        scratch_shapes=[pltpu.VMEM((2, d, de), F32), pltpu.VMEM((2, d, de), F32), pltpu.VMEM((2, de, d), F32),
                        pltpu.VMEM((d, 2 * de), BF16), pltpu.VMEM((de, d), BF16),
                        pltpu.SemaphoreType.DMA((2, 3)), pltpu.SMEM((1,), I32)])
    return pl.pallas_call(
        _expert_kernel, grid_spec=grid_spec,
        out_shape=jax.ShapeDtypeStruct((rows, d // 2), U32),
        compiler_params=_cparams(("arbitrary",)), name="experts",
    )(blk_e, n_active, next_e, xs, w_gate, w_up, w_down)


def _combine_ln_kernel(src_ref, h1_ref, h1p_ref, gw_ref, ys_hbm, wg_ref, wu_ref, wd_ref, lg_ref, lb_ref,
                       yp_ref, ysm_ref, g_s, sem, *, ntp):
    i = pl.program_id(0)
    tm = h1_ref.shape[0]

    def issue(t, _):
        for k in range(TOP_K):
            r = src_ref[t * TOP_K + k]
            pltpu.make_async_copy(ys_hbm.at[pl.ds(r, 1), :], g_s.at[k, pl.ds(t, 1), :], sem).start(priority=k % 2)
        return 0

    lax.fori_loop(0, tm, issue, 0)
    xb = _unpack_halves(h1p_ref[...]).astype(BF16)
    hb = (_silu(_dot(xb, wg_ref[...])) * _dot(xb, wu_ref[...])).astype(BF16)
    f = _dot(hb, wd_ref[...])
    for k in range(TOP_K):
        pltpu.make_async_copy(ys_hbm.at[pl.ds(0, tm), :], g_s.at[k], sem).wait()
    gw = gw_ref[...]
    for k in range(TOP_K):
        f = f + gw[:, k:k + 1] * _unpack_halves(g_s[k])
    y = _layer_norm_rows(ALPHA * h1_ref[...] + f, lg_ref[...], lb_ref[...])

    @pl.when(i < ntp)
    def _():
        yp_ref[...] = y

    @pl.when(i == ntp)
    def _():
        ysm_ref[...] = y


def _combine_ln(src_flat, h1, h1p, gw, ys, ws_gate, ws_up, ws_down, ln_g, ln_b, *, tm):
    n_all, d = h1.shape
    ntp = n_all // tm - 1
    kern = functools.partial(_combine_ln_kernel, ntp=ntp)
    return pl.pallas_call(
        kern, grid=(ntp + 1,),
        in_specs=[pl.BlockSpec((tm * TOP_K,), lambda i: (i,), memory_space=pltpu.SMEM),
                  pl.BlockSpec((tm, d), lambda i: (i, 0)), pl.BlockSpec((tm, d // 2), lambda i: (i, 0)),
                  pl.BlockSpec((tm, TOP_K), lambda i: (i, 0)), pl.BlockSpec(memory_space=pl.ANY),
                  _const_spec(ws_gate.shape), _const_spec(ws_up.shape),
                  _const_spec(ws_down.shape), _const_spec(ln_g.shape), _const_spec(ln_b.shape)],
        out_specs=[pl.BlockSpec((tm, d), lambda i: (jnp.minimum(i, ntp - 1), 0)),
                   pl.BlockSpec((tm, d), lambda i: (0, 0))],
        out_shape=[jax.ShapeDtypeStruct((ntp * tm, d), F32), jax.ShapeDtypeStruct((tm, d), F32)],
        scratch_shapes=[pltpu.VMEM((TOP_K, tm, d // 2), U32), pltpu.SemaphoreType.DMA(())],
        compiler_params=_cparams(("arbitrary",)), name="combine_ln",
    )(src_flat, h1, h1p, gw, ys, ws_gate, ws_up, ws_down, ln_g, ln_b)


def _prep_weights(w_in, b_gate, conv_w, conv_b, norm_g, w_out, ln1_g, ln1_b, w_router, router_bias,
                  w_s_gate, w_s_up, w_s_down, ln2_g, ln2_b):
    d = w_in.shape[0]
    n_main = QK_COLS + 2 * VM_COLS + 3 * FOX_HEAD_COLS
    n_gate = w_in.shape[1] - n_main
    w_main = w_in[:, :n_main].astype(BF16)
    wg = jnp.zeros((d, LANES), F32).at[:, :n_gate].set(w_in[:, n_main:])
    bg = jnp.zeros((LANES,), F32).at[:n_gate].set(b_gate)
    for c in FOX_GATE_COPIES[1:]:
        wg = wg.at[:, c:c + HF].set(w_in[:, n_main + 2 * HM:])
        bg = bg.at[c:c + HF].set(b_gate[2 * HM:])
    wg = wg.astype(BF16)
    wr_hi = w_router.T.astype(BF16)
    wr_lo = (w_router.T - wr_hi.astype(F32)).astype(BF16)
    return dict(
        w_main=w_main, w_gate=wg, w_gate_t=wg.T, bg_row=bg[None, :], bg_col=bg[:, None],
        conv_w8=jnp.zeros((SUBLANES, QK_COLS), F32).at[:CONV_W].set(conv_w), conv_b=conv_b[None, :],
        norm_g=norm_g[None, :], w_out=w_out.astype(BF16), ln1_g=ln1_g[None, :], ln1_b=ln1_b[None, :],
        wr_hi=wr_hi, wr_lo=wr_lo, rbias=router_bias[:, None],
        ws_gate=w_s_gate.astype(BF16), ws_up=w_s_up.astype(BF16), ws_down=w_s_down.astype(BF16),
        ln2_g=ln2_g[None, :], ln2_b=ln2_b[None, :])


def _pad_rows(a, bsz, t, tp, value=0.0):
    w = a.shape[-1]
    a = a.reshape(bsz, t, w)
    pad = jnp.full((bsz, tp - t, w), value, a.dtype) if not hasattr(value, "shape") else \
        jnp.broadcast_to(value.astype(a.dtype), (bsz, tp - t, w))
    return jnp.concatenate([a, pad], axis=1).reshape(bsz * tp, w)


def _mixer_prompt(x, wts, *, tm, chunk, tb):
    bsz, t, d = x.shape
    n = bsz * t
    x2 = x.reshape(n, d)
    uqk, vm, om, qt, kf, vf, kb, vt, g, gt = _inproj(
        x2, wts["w_main"], wts["w_gate"], wts["w_gate_t"], wts["bg_row"], wts["bg_col"],
        tm=tm, seq_len=t, prompt=True)
    fcols, lf = _fox_cumsum(g, bsz, t, min(512, t))
    zeros = lambda *s: jnp.zeros(s, F32)
    hm, c1, n1, m1 = _mlstm(uqk, vm, om, g, gt, zeros(bsz, SUBLANES, QK_COLS), wts["conv_w8"], wts["conv_b"],
                            wts["norm_g"], zeros(bsz, HM, DQK, DV), zeros(bsz, HM, 1, DQK),
                            zeros(bsz, HM, 1, LANES), bsz=bsz, seq_len=t, chunk=chunk)
    of = _fox_prompt(qt, kb, vt, fcols, tb=tb).reshape(n, FOX_HEAD_COLS)
    state = (kf.reshape(bsz, t, HF, DHF), vf.reshape(bsz, t, HF, DHF),
             lf[:, 2 * HM:2 * HM + HF].reshape(bsz, t, HF), c1, n1[:, :, 0, :], m1[:, :, 0, 0],
             uqk.reshape(bsz, t, QK_COLS)[:, t - (CONV_W - 1):, :])
    return x2, hm, of, state


def _mixer_sample(x, conv_hist, c0, n0, m0, ck, cv, clogf, wts, *, chunk, kb, cache_off):
    bsz, t, d = x.shape
    n = bsz * t
    past = ck.shape[1]
    x2 = x.reshape(n, d)
    uqk, vm, om, qf, kf, vf, g, gt = _inproj(
        x2, wts["w_main"], wts["w_gate"], wts["w_gate_t"], wts["bg_row"], wts["bg_col"],
        tm=n, seq_len=t, prompt=False)
    gpad = jnp.concatenate([jnp.full((HM,), NEG_BIG, F32), jnp.full((HM,), -NEG_BIG, F32),
                            jnp.zeros((LANES - 2 * HM,), F32)])
    g_p = _pad_rows(g, bsz, t, chunk, gpad)
    hist8 = jnp.concatenate([jnp.zeros((bsz, SUBLANES - (CONV_W - 1), QK_COLS), F32), conv_hist], axis=1)
    hm_p, c1, n1, m1 = _mlstm(
        _pad_rows(uqk, bsz, t, chunk), _pad_rows(vm, bsz, t, chunk), _pad_rows(om, bsz, t, chunk),
        g_p, g_p.T, hist8, wts["conv_w8"], wts["conv_b"], wts["norm_g"],
        c0, n0[:, :, None, :], jnp.broadcast_to(m0[:, :, None, None], (bsz, HM, 1, LANES)),
        bsz=bsz, seq_len=chunk, chunk=chunk)
    hm = hm_p.reshape(bsz, chunk, VM_COLS)[:, :t].reshape(n, VM_COLS)
    lf_new = _log_sigmoid_rows(g[:, 2 * HM:2 * HM + HF])
    q4 = qf.reshape(bsz, t, HF, DHF)
    eye = jnp.eye(HF, dtype=BF16)
    qbd = jnp.einsum("bthd,hg->bhdgt", q4, eye).reshape(bsz, FOX_HEAD_COLS, HF * t)
    rows_new = LANES
    expand = lambda a: jnp.repeat(a, t, axis=-1)
    pad3 = lambda a: jnp.concatenate(
        [a, jnp.zeros((bsz, rows_new - t, a.shape[-1]), a.dtype)], axis=1)
    of = _fox_sample(qbd, ck.reshape(-1, past, FOX_HEAD_COLS), cv.reshape(-1, past, FOX_HEAD_COLS),
                     expand(clogf), pad3(kf.reshape(bsz, t, FOX_HEAD_COLS)),
                     pad3(vf.reshape(bsz, t, FOX_HEAD_COLS)), pad3(expand(lf_new.reshape(bsz, t, HF))),
                     kb=kb, n_new=t, cache_off=cache_off).reshape(n, FOX_HEAD_COLS)
    state = (kf.reshape(bsz, t, HF, DHF), vf.reshape(bsz, t, HF, DHF), lf_new.reshape(bsz, t, HF),
             c1, n1[:, :, 0, :], m1[:, :, 0, 0],
             jnp.concatenate([conv_hist, uqk.reshape(bsz, t, QK_COLS)], axis=1)[:, -(CONV_W - 1):, :])
    return x2, hm, of, state


def _log_sigmoid_rows(a):
    n, w = a.shape
    ap = jnp.zeros((n, LANES), F32).at[:, :w].set(a)

    def kern(a_ref, o_ref):
        o_ref[...] = _log_sigmoid(a_ref[...])

    out = pl.pallas_call(kern, out_shape=jax.ShapeDtypeStruct((n, LANES), F32), name="log_sigmoid")(ap)
    return out[:, :w]


def _dispatch_tables(eidx, rank, counts, n_rows_total):
    blk = EXPERT_BLOCK
    pcounts = (counts + blk - 1) // blk * blk
    pend = jnp.cumsum(pcounts)
    pstart = pend - pcounts
    experts = jnp.arange(N_EXPERTS, dtype=I32)
    dest = jnp.sum(jnp.where(eidx[..., None] == experts, pstart, 0), axis=-1) + rank
    nb = n_rows_total // blk
    idx = jnp.arange(nb, dtype=I32)
    blk_e = jnp.minimum(jnp.sum(pend[None, :] <= (idx * blk)[:, None], axis=1), N_EXPERTS - 1).astype(I32)
    n_active = (pend[-1] // blk).astype(I32)
    starts = jnp.concatenate([jnp.ones((1,), bool), blk_e[1:] != blk_e[:-1]])
    next_start = lax.cummin(jnp.where(starts, idx, nb), reverse=True)
    next_start = jnp.concatenate([next_start[1:], jnp.full((1,), nb, I32)])
    next_e = jnp.where(next_start < n_active, blk_e[jnp.minimum(next_start, nb - 1)], -1).astype(I32)
    last_of_run = jnp.concatenate([starts[1:], jnp.ones((1,), bool)])
    zflag = ((idx >= n_active - 1) | last_of_run).astype(I32)
    return dest.astype(I32), blk_e, n_active.reshape(1), next_e, zflag


def kernel(x_prompt, x_sample, cache_fox_k, cache_fox_v, cache_fox_logf, state_mlstm_C, state_mlstm_n,
           state_mlstm_m, state_conv, w_in, b_gate, conv_w, conv_b, mlstm_norm_g, w_out, ln1_g, ln1_b,
           w_router, router_bias, w_e_gate, w_e_up, w_e_down, w_s_gate, w_s_up, w_s_down, ln2_g, ln2_b):
    l = 0
    bp, tp, d = x_prompt.shape
    bs, ts, _ = x_sample.shape
    n_p, n_s = bp * tp, bs * ts
    n_tot = n_p + n_s
    wts = _prep_weights(w_in[l], b_gate[l], conv_w[l], conv_b[l], mlstm_norm_g[l], w_out[l], ln1_g[l], ln1_b[l],
                        w_router[l], router_bias[l], w_s_gate[l], w_s_up[l], w_s_down[l], ln2_g[l], ln2_b[l])

    tm_p = min(256, tp)
    xp2, hm_p, of_p, st_p = _mixer_prompt(x_prompt, wts, tm=tm_p, chunk=min(256, tp), tb=min(512, tp))
    xs2, hm_s, of_s, st_s = _mixer_sample(
        x_sample, state_conv[l], state_mlstm_C[l], state_mlstm_n[l], state_mlstm_m[l],
        cache_fox_k.reshape((-1,) + cache_fox_k.shape[2:]), cache_fox_v.reshape((-1,) + cache_fox_v.shape[2:]),
        cache_fox_logf[l], wts, chunk=LANES, kb=min(512, cache_fox_k.shape[2]), cache_off=l * bs)

    tm = tm_p
    padr = lambda a: jnp.concatenate([a, jnp.zeros((tm - n_s, a.shape[1]), a.dtype)], axis=0)
    h1, h1p, eidx_t, gw_t, rank_t, cnt = _outproj_router(
        hm_p, of_p, xp2, padr(hm_s), padr(of_s), padr(xs2),
        wts["w_out"], wts["ln1_g"], wts["ln1_b"], wts["wr_hi"], wts["wr_lo"], wts["rbias"],
        tm=tm, n_valid=n_tot)
    n_all = n_p + tm
    n_pad = n_all - n_tot
    eidx = eidx_t[:, :n_tot].T
    gw = gw_t[:, :n_tot].T
    rank = rank_t[:, :n_tot].T
    counts = cnt[:, 0].astype(I32)

    nk = n_tot * TOP_K
    n_blocks = min(N_EXPERTS, nk) + nk // EXPERT_BLOCK
    rows_total = n_blocks * EXPERT_BLOCK
    dest, blk_e, n_active, next_e, zflag = _dispatch_tables(eidx, rank, counts, rows_total)
    park = rows_total + jnp.arange(n_pad * TOP_K, dtype=I32).reshape(n_pad, TOP_K)
    xs = _dispatch(h1p, jnp.concatenate([dest, park], axis=0).reshape(-1), zflag,
                   tm=tm, rows_alloc=rows_total + n_pad * TOP_K)
    de = w_e_gate.shape[-1]
    ys = _experts(xs, w_e_gate.reshape(-1, d, de), w_e_up.reshape(-1, d, de), w_e_down.reshape(-1, de, d),
                  blk_e + l * N_EXPERTS, n_active, jnp.where(next_e >= 0, next_e + l * N_EXPERTS, -1))
    src = jnp.concatenate([dest, jnp.zeros((n_pad, TOP_K), I32)], axis=0)
    gw_all = jnp.concatenate([gw, jnp.zeros((n_pad, TOP_K), F32)], axis=0)
    y_p, y_s = _combine_ln(src.reshape(-1), h1, h1p, gw_all, ys, wts["ws_gate"], wts["ws_up"], wts["ws_down"],
                           wts["ln2_g"], wts["ln2_b"], tm=tm)
    y_p = y_p.reshape(bp, tp, d)
    y_s = y_s[:n_s].reshape(bs, ts, d)

    stack = lambda s: tuple(a[None] for a in s)
    return (y_p, y_s) + stack(st_p) + stack(st_s)
```

```python
import functools
import math

import jax
import jax.numpy as jnp
from jax import lax
from jax.experimental import pallas as pl
from jax.experimental.pallas import tpu as pltpu

F32 = jnp.float32
BF16 = jnp.bfloat16
I32 = jnp.int32
U32 = jnp.uint32

HM, DQK, DV = 4, 128, 256
HF, DHF = 8, 128
CONV_W = 4
QK_COLS = 2 * HM * DQK
VM_COLS = HM * DV
FOX_HEAD_COLS = HF * DHF
N_EXPERTS, TOP_K, N_GROUPS, TOP_GROUPS = 256, 8, 8, 4
GROUP_SIZE = N_EXPERTS // N_GROUPS
ROUTE_SCALE = 2.5
DEPTH = 1
ALPHA = (2 * DEPTH) ** 0.25
LN_EPS = 1e-5
HEAD_EPS = 1e-6
LOG2E = math.log2(math.e)

LANES = 128
SUBLANES = 8
VMEM_LIMIT_BYTES = 56 * 1024 * 1024

NEG_BIG = -1e30
EXPERT_BLOCK = 256
ROUTER_TILE = 512


def _dot(a, b):
    return jnp.dot(a, b, preferred_element_type=F32)


def _dot_nt(a, b):
    return lax.dot_general(a, b, (((1,), (1,)), ((), ())), preferred_element_type=F32)


def _dot_tn(a, b):
    return lax.dot_general(a, b, (((0,), (0,)), ((), ())), preferred_element_type=F32)


def _split3(x):
    hi = x.astype(BF16)
    r1 = x - hi.astype(F32)
    mid = r1.astype(BF16)
    lo = (r1 - mid.astype(F32)).astype(BF16)
    return hi, mid, lo


def _split2(x):
    hi = x.astype(BF16)
    lo = (x - hi.astype(F32)).astype(BF16)
    return hi, lo


def _pack_halves(x):
    c = x.shape[1] // 2
    lo = lax.bitcast_convert_type(x[:, :c].astype(BF16).astype(F32), U32)
    hi = lax.bitcast_convert_type(x[:, c:].astype(BF16).astype(F32), U32)
    return (lo >> 16) | hi


def _unpack_halves(p):
    lo = lax.bitcast_convert_type(p << 16, F32)
    hi = lax.bitcast_convert_type(p & jnp.uint32(0xFFFF0000), F32)
    return jnp.concatenate([lo, hi], axis=1)


def _log_sigmoid(x):
    return jnp.minimum(x, 0.0) - jnp.log1p(jnp.exp(-jnp.abs(x)))


def _silu(x):
    return x * jax.nn.sigmoid(x)


def _cparams(sem):
    return pltpu.CompilerParams(dimension_semantics=sem, vmem_limit_bytes=VMEM_LIMIT_BYTES)


def _const_spec(shape):
    nd = len(shape)
    return pl.BlockSpec(shape, lambda *_: (0,) * nd, pipeline_mode=pl.Buffered(1))


def _inproj_kernel(x_ref, w_ref, wg_ref, wgt_ref, bg_ref, bgt_ref, *outs, prompt, q_scale):
    xb = x_ref[...].astype(BF16)
    if prompt:
        uqk_ref, vm_ref, om_ref, qf_ref, kf_ref, vf_ref, kb_ref, vt_ref, g_ref, gt_ref = outs
    else:
        uqk_ref, vm_ref, om_ref, qf_ref, kf_ref, vf_ref, g_ref, gt_ref = outs
    o = 0
    uqk_ref[...] = _dot(xb, w_ref[:, o:o + QK_COLS])
    o += QK_COLS
    vm_ref[...] = _dot(xb, w_ref[:, o:o + VM_COLS]).astype(BF16)
    o += VM_COLS
    om_ref[...] = jax.nn.sigmoid(_dot(xb, w_ref[:, o:o + VM_COLS])).astype(BF16)
    o += VM_COLS
    q = _dot(xb, w_ref[:, o:o + FOX_HEAD_COLS]) * q_scale
    o += FOX_HEAD_COLS
    k = _dot(xb, w_ref[:, o:o + FOX_HEAD_COLS])
    o += FOX_HEAD_COLS
    v = _dot(xb, w_ref[:, o:o + FOX_HEAD_COLS])
    kf_ref[...] = k
    vf_ref[...] = v
    if prompt:
        qf_ref[0] = q.T.astype(BF16)
        vt_ref[0] = v.T.astype(BF16)
        for h in range(HF):
            kb_ref[0, h] = k[:, h * DHF:(h + 1) * DHF].astype(BF16)
    else:
        qf_ref[...] = q.astype(BF16)
    g_ref[...] = _dot(xb, wg_ref[...]) + bg_ref[...]
    gt_ref[...] = _dot_nt(wgt_ref[...], xb) + bgt_ref[...]


def _inproj(x2, w_main, w_gate, w_gate_t, b_gate_row, b_gate_col, *, tm, seq_len, prompt):
    n, d = x2.shape
    nt = n // tm
    grid = (nt,)
    row = lambda w: pl.BlockSpec((tm, w), lambda i: (i, 0))
    out_shape = [jax.ShapeDtypeStruct((n, QK_COLS), F32),
                 jax.ShapeDtypeStruct((n, VM_COLS), BF16),
                 jax.ShapeDtypeStruct((n, VM_COLS), BF16)]
    out_specs = [row(QK_COLS), row(VM_COLS), row(VM_COLS)]
    f32_rows = jax.ShapeDtypeStruct((n, FOX_HEAD_COLS), F32)
    if prompt:
        bsz = n // seq_len
        tpb = seq_len // tm
        t_shape = jax.ShapeDtypeStruct((bsz, FOX_HEAD_COLS, seq_len), BF16)
        t_spec = pl.BlockSpec((1, FOX_HEAD_COLS, tm), lambda i: (i // tpb, 0, i % tpb))
        out_shape += [t_shape, f32_rows, f32_rows,
                      jax.ShapeDtypeStruct((bsz, HF, seq_len, DHF), BF16), t_shape]
        out_specs += [t_spec, row(FOX_HEAD_COLS), row(FOX_HEAD_COLS),
                      pl.BlockSpec((1, HF, tm, DHF), lambda i: (i // tpb, 0, i % tpb, 0)), t_spec]
    else:
        out_shape += [jax.ShapeDtypeStruct((n, FOX_HEAD_COLS), BF16), f32_rows, f32_rows]
        out_specs += [row(FOX_HEAD_COLS), row(FOX_HEAD_COLS), row(FOX_HEAD_COLS)]
    out_shape += [jax.ShapeDtypeStruct((n, LANES), F32), jax.ShapeDtypeStruct((LANES, n), F32)]
    out_specs += [row(LANES), pl.BlockSpec((LANES, tm), lambda i: (0, i))]
    kern = functools.partial(_inproj_kernel, prompt=prompt, q_scale=DHF ** -0.5 * LOG2E)
    return pl.pallas_call(
        kern, grid=grid,
        in_specs=[row(d), _const_spec(w_main.shape), _const_spec(w_gate.shape),
                  _const_spec(w_gate_t.shape), _const_spec(b_gate_row.shape), _const_spec(b_gate_col.shape)],
        out_specs=out_specs, out_shape=out_shape,
        compiler_params=_cparams(("arbitrary",)), name="inproj",
    )(x2, w_main, w_gate, w_gate_t, b_gate_row, b_gate_col)


def _fox_cumsum_kernel(g_ref, fc_ref, lf_ref, carry_s):
    @pl.when(pl.program_id(1) == 0)
    def _():
        carry_s[...] = jnp.zeros(carry_s.shape, F32)

    lf = _log_sigmoid(g_ref[...])
    lf_ref[...] = lf
    r = lf.shape[0]
    tril = (lax.broadcasted_iota(I32, (r, r), 1) <= lax.broadcasted_iota(I32, (r, r), 0)).astype(BF16)
    cs = carry_s[...] + sum(_dot(tril, p) for p in _split3(lf))
    carry_s[...] = cs[r - 1:r, :]
    fc_ref[...] = cs * LOG2E


def _fox_cumsum(g, bsz, seq_len, rows):
    nc = seq_len // rows
    spec = pl.BlockSpec((rows, LANES), lambda b, c: (b * nc + c, 0))
    shape = jax.ShapeDtypeStruct((bsz * seq_len, LANES), F32)
    return pl.pallas_call(
        _fox_cumsum_kernel, grid=(bsz, nc),
        in_specs=[spec], out_specs=[spec, spec], out_shape=[shape, shape],
        scratch_shapes=[pltpu.VMEM((1, LANES), F32)],
        compiler_params=_cparams(("arbitrary", "arbitrary")), name="fox_cumsum",
    )(g)


def _mlstm_kernel(uqk_ref, v_ref, om_ref, g_ref, gt_ref, hist_ref, cw_ref, cb_ref, ng_ref,
                  c0_ref, n0_ref, m0_ref, h_ref, c1_ref, n1_ref, m1_ref,
                  c_s, n_s, m_s, prev_s):
    c_idx = pl.program_id(1)
    nc = pl.num_programs(1)
    L = uqk_ref.shape[0]

    @pl.when(c_idx == 0)
    def _():
        c_s[...] = c0_ref[0]
        n_s[...] = n0_ref[0]
        m_s[...] = m0_ref[0]
        prev_s[...] = hist_ref[0]

    z = uqk_ref[...]
    ext = jnp.concatenate([prev_s[...], z], axis=0)
    prev_s[...] = z[L - SUBLANES:, :]
    cw = cw_ref[...]
    acc = cb_ref[...] + ext[SUBLANES:SUBLANES + L] * cw[CONV_W - 1:CONV_W]
    for j in range(CONV_W - 1):
        off = SUBLANES - (CONV_W - 1) + j
        acc = acc + ext[off:off + L] * cw[j:j + 1]
    qk = _silu(acc)

    g = g_ref[...]
    gt = gt_ref[...]
    lsg = _log_sigmoid(g)
    lsgt = _log_sigmoid(gt)
    ri = lax.broadcasted_iota(I32, (L, L), 0)
    ci = lax.broadcasted_iota(I32, (L, L), 1)
    causal = ci <= ri
    tril = causal.astype(BF16)
    triu = (ri <= ci).astype(BF16)
    b_cols = sum(_dot(tril, p) for p in _split3(lsg))
    b_rows = sum(_dot(p, triu) for p in _split3(lsgt))

    for h in range(HM):
        q = qk[:, h * DQK:(h + 1) * DQK]
        k = qk[:, HM * DQK + h * DQK:HM * DQK + (h + 1) * DQK] * (DQK ** -0.5)
        qb = q.astype(BF16)
        kb = k.astype(BF16)
        v = v_ref[:, h * DV:(h + 1) * DV]
        b_col = b_cols[:, HM + h:HM + h + 1]
        b_row = b_rows[HM + h:HM + h + 1, :]
        li_col = g[:, h:h + 1]
        li_row = gt[h:h + 1, :]
        m_prev = m_s[h][:, 0:1]
        c_prev = c_s[h]
        n_prev = n_s[h]

        c_row = li_row - b_row
        dmat = jnp.where(causal, b_col + c_row, -jnp.inf)
        inter = b_col + m_prev
        m_t = jnp.maximum(inter, jnp.max(dmat, axis=1, keepdims=True))
        w_intra = jnp.exp(dmat - m_t)
        w_state = jnp.exp(inter - m_t)
        s = _dot_nt(qb, kb) * w_intra
        num = _dot(s.astype(BF16), v) + w_state * _dot(qb, c_prev.astype(BF16))
        den = jnp.sum(s, axis=1, keepdims=True) + w_state * jnp.sum(q * n_prev, axis=1, keepdims=True)
        hh = num / jnp.maximum(jnp.abs(den), jnp.exp(-m_t))

        b_end = b_col[L - 1:L, :]
        a_col = b_end - b_col + li_col
        a_row = b_end + c_row
        m_new = jnp.maximum(b_end + m_prev, jnp.max(a_row, axis=1, keepdims=True))
        decay = jnp.exp(b_end + m_prev - m_new)
        wk = jnp.exp(a_col - m_new) * k
        c_s[h] = decay * c_prev + _dot_tn(wk.astype(BF16), v)
        n_s[h] = decay * n_prev + jnp.sum(wk, axis=0, keepdims=True)
        m_s[h] = jnp.broadcast_to(m_new, (1, LANES))

        mu = jnp.mean(hh, axis=1, keepdims=True)
        hc = hh - mu
        var = jnp.mean(hc * hc, axis=1, keepdims=True)
        hn = hc * lax.rsqrt(var + HEAD_EPS)
        sl = slice(h * DV, (h + 1) * DV)
        h_ref[:, sl] = (om_ref[:, sl].astype(F32) * hn * ng_ref[:, sl]).astype(BF16)

    @pl.when(c_idx == nc - 1)
    def _():
        c1_ref[0] = c_s[...]
        n1_ref[0] = n_s[...]
        m1_ref[0] = m_s[...]


def _mlstm(uqk, vm, om, g, g_t, hist8, conv_w8, conv_b, norm_g, c0, n0, m0, *, bsz, seq_len, chunk):
    nc = seq_len // chunk
    rowspec = lambda w: pl.BlockSpec((chunk, w), lambda b, c: (b * nc + c, 0))
    st = lambda shape: pl.BlockSpec((1,) + shape, lambda b, c: (b,) + (0,) * len(shape))
    return pl.pallas_call(
        _mlstm_kernel, grid=(bsz, nc),
        in_specs=[rowspec(QK_COLS), rowspec(VM_COLS), rowspec(VM_COLS), rowspec(LANES),
                  pl.BlockSpec((SUBLANES, chunk), lambda b, c: (0, b * nc + c)),
                  st((SUBLANES, QK_COLS)),
                  pl.BlockSpec((SUBLANES, QK_COLS), lambda b, c: (0, 0)),
                  pl.BlockSpec((1, QK_COLS), lambda b, c: (0, 0)),
                  pl.BlockSpec((1, VM_COLS), lambda b, c: (0, 0)),
                  st((HM, DQK, DV)), st((HM, 1, DQK)), st((HM, 1, LANES))],
        out_specs=[rowspec(VM_COLS), st((HM, DQK, DV)), st((HM, 1, DQK)), st((HM, 1, LANES))],
        out_shape=[jax.ShapeDtypeStruct((bsz * seq_len, VM_COLS), BF16),
                   jax.ShapeDtypeStruct((bsz, HM, DQK, DV), F32),
                   jax.ShapeDtypeStruct((bsz, HM, 1, DQK), F32),
                   jax.ShapeDtypeStruct((bsz, HM, 1, LANES), F32)],
        scratch_shapes=[pltpu.VMEM((HM, DQK, DV), F32), pltpu.VMEM((HM, 1, DQK), F32),
                        pltpu.VMEM((HM, 1, LANES), F32), pltpu.VMEM((SUBLANES, QK_COLS), F32)],
        compiler_params=_cparams(("arbitrary", "arbitrary")), name="mlstm",
    )(uqk, vm, om, g, g_t, hist8, conv_w8, conv_b, norm_g, c0, n0, m0)


FOX_GATE_COPIES = (8, 24, 40)


def _fox_prompt_kernel(qt_ref, k_ref, vt_ref, fc_ref, o_ref, kaug_s, u_s, p_s, acc_s, st_s, *, tb, prep_rows):
    h = pl.program_id(1)
    i = pl.program_id(2)
    seq = k_ref.shape[2]

    @pl.when(i == 0)
    def _():
        lane = lax.broadcasted_iota(I32, (prep_rows, LANES), 1)
        zero = jnp.zeros((prep_rows, LANES), BF16)

        def prep(c, _):
            rs = pl.ds(pl.multiple_of(c * prep_rows, prep_rows), prep_rows)
            hi, mid, lo = _split3(-fc_ref[rs, :])
            aug = jnp.where(lane == FOX_GATE_COPIES[0] + h, hi,
                            jnp.where(lane == FOX_GATE_COPIES[1] + h, mid,
                                      jnp.where(lane == FOX_GATE_COPIES[2] + h, lo, zero)))
            kaug_s[rs, 0:DHF] = k_ref[0, 0, rs, :]
            kaug_s[rs, DHF:2 * DHF] = aug
            return 0

        lax.fori_loop(0, seq // prep_rows, prep, 0)

    row = lax.broadcasted_iota(I32, (DHF, tb), 0)
    ones = jnp.where(row == FOX_GATE_COPIES[0] + h, 1.0,
                     jnp.where(row == FOX_GATE_COPIES[1] + h, 1.0,
                               jnp.where(row == FOX_GATE_COPIES[2] + h, 1.0, 0.0))).astype(BF16)
    qaug = jnp.concatenate([qt_ref[0], ones], axis=0)

    def key_rows(r):
        return pl.ds(pl.multiple_of(jnp.maximum(i - r, 0) * tb, tb), tb)

    def scores(r, slot):
        u_s[slot] = _dot(kaug_s[key_rows(r), :], qaug)

    def softmax(slot, diagonal):
        for c in range(tb // LANES):
            cs = slice(c * LANES, (c + 1) * LANES)
            u = u_s[slot, :, cs]
            if diagonal:
                kpos = lax.broadcasted_iota(I32, (tb, LANES), 0)
                qpos = lax.broadcasted_iota(I32, (tb, LANES), 1) + c * LANES
                u = jnp.where(kpos <= qpos, u, -jnp.inf)
            m = st_s[0:1, cs]
            m_new = jnp.maximum(m, jnp.max(u, axis=0, keepdims=True))
            alpha = jnp.exp2(m - m_new)
            p = jnp.exp2(u - m_new)
            st_s[0:1, cs] = m_new
            st_s[1:2, cs] = alpha * st_s[1:2, cs] + jnp.sum(p, axis=0, keepdims=True)
            st_s[2:3, cs] = alpha
            p_s[slot, :, cs] = p.astype(BF16)

    def accumulate(slot, r):
        acc_s[...] = st_s[2:3, :] * acc_s[...] + _dot(vt_ref[0, :, key_rows(r)], p_s[slot])

    def step(r, slot, diagonal=False, first=False, last=False):
        if not last:
            scores(r + 1, 1 - slot)
        if not first:
            accumulate(1 - slot, r - 1)
        softmax(slot, diagonal)

    def pair(s, carry):
        step(2 * s + 1, 1)
        step(2 * s + 2, 0)
        return carry

    acc_s[...] = jnp.zeros(acc_s.shape, F32)
    st_s[...] = jnp.concatenate([jnp.full((1, tb), -jnp.inf, F32), jnp.zeros((SUBLANES - 1, tb), F32)], axis=0)
    scores(0, 0)
    step(0, 0, diagonal=True, first=True)
    lax.fori_loop(0, i // 2, pair, 0)

    @pl.when(i % 2 == 1)
    def _():
        step(i, 1, last=True)
        accumulate(1, i)

    @pl.when(i % 2 == 0)
    def _():
        accumulate(0, i)

    o_ref[0] = (acc_s[...] / st_s[1:2, :]).T.astype(o_ref.dtype)


def _fox_prompt(q_t, k_hm, v_t, fcols, *, tb):
    bsz, _, seq_len, _ = k_hm.shape
    kern = functools.partial(_fox_prompt_kernel, tb=tb, prep_rows=min(512, seq_len))
    return pl.pallas_call(
        kern, grid=(bsz, HF, seq_len // tb),
        in_specs=[pl.BlockSpec((1, DHF, tb), lambda b, h, i: (b, h, i)),
                  pl.BlockSpec((1, 1, seq_len, DHF), lambda b, h, i: (b, h, 0, 0)),
                  pl.BlockSpec((1, DHF, seq_len), lambda b, h, i: (b, h, 0)),
                  pl.BlockSpec((seq_len, LANES), lambda b, h, i: (b, 0))],
        out_specs=pl.BlockSpec((1, tb, DHF), lambda b, h, i: (b, i, h)),
        out_shape=jax.ShapeDtypeStruct((bsz, seq_len, FOX_HEAD_COLS), BF16),
        scratch_shapes=[pltpu.VMEM((seq_len, 2 * DHF), BF16), pltpu.VMEM((2, tb, tb), F32),
                        pltpu.VMEM((2, tb, tb), BF16), pltpu.VMEM((DHF, tb), F32),
                        pltpu.VMEM((SUBLANES, tb), F32)],
        compiler_params=_cparams(("arbitrary", "arbitrary", "arbitrary")), name="fox_prompt",
    )(q_t, k_hm, v_t, fcols)


def _fox_sample_kernel(qbd_ref, ck_ref, cv_ref, clf_ref, kn_ref, vn_ref, lfn_ref, o_ref,
                       m_s, l_s, acc_s, f_s, *, kb, n_new):
    j = pl.program_id(1)
    nj = pl.num_programs(1)

    @pl.when(j == 0)
    def _():
        m_s[...] = jnp.full(m_s.shape, -jnp.inf, F32)
        l_s[...] = jnp.zeros(l_s.shape, F32)
        acc_s[...] = jnp.zeros(acc_s.shape, F32)
        f_s[...] = jnp.zeros(f_s.shape, F32)

    qbd = qbd_ref[0]
    eye = (lax.broadcasted_iota(I32, (LANES, LANES), 0) == lax.broadcasted_iota(I32, (LANES, LANES), 1))

    def to_col(row):
        return jnp.sum(jnp.where(eye, row, 0.0), axis=1, keepdims=True)

    def step(kf32, vf32, lf, valid):
        rows = kf32.shape[0]
        ri = lax.broadcasted_iota(I32, (rows, rows), 0)
        ci = lax.broadcasted_iota(I32, (rows, rows), 1)
        tril = (ci <= ri).astype(BF16)
        fk = f_s[...] + sum(_dot(tril, p) for p in _split3(lf))
        f_s[...] = fk[rows - 1:rows, :]
        u = _dot(kf32.astype(BF16), qbd) - fk * LOG2E
        if valid is not None:
            u = jnp.where(valid, u, -jnp.inf)
        m = m_s[...]
        m_new = jnp.maximum(m, jnp.max(u, axis=0, keepdims=True))
        alpha = jnp.exp2(m - m_new)
        p = jnp.exp2(u - m_new)
        l_s[...] = alpha * l_s[...] + jnp.sum(p, axis=0, keepdims=True)
        m_s[...] = m_new
        acc_s[...] = to_col(alpha) * acc_s[...] + _dot_tn(p.astype(BF16), vf32.astype(BF16))

    step(ck_ref[0], cv_ref[0], clf_ref[0], None)

    @pl.when(j == nj - 1)
    def _():
        rows = kn_ref.shape[1]
        r = lax.broadcasted_iota(I32, (rows, LANES), 0)
        t = lax.broadcasted_iota(I32, (rows, LANES), 1) % n_new
        step(kn_ref[0], vn_ref[0], lfn_ref[0], r <= t)
        inv = 1.0 / to_col(l_s[...])
        acc = acc_s[...] * inv
        for h in range(HF):
            o_ref[0, :, h * DHF:(h + 1) * DHF] = acc[h * n_new:(h + 1) * n_new,
                                                     h * DHF:(h + 1) * DHF].astype(o_ref.dtype)


def _fox_sample(qbd, ck, cv, clf_exp, k_new, v_new, lf_new_exp, *, kb, n_new, cache_off):
    bsz = qbd.shape[0]
    past = ck.shape[1]
    rows_new = k_new.shape[1]
    kern = functools.partial(_fox_sample_kernel, kb=kb, n_new=n_new)
    blk = lambda w: pl.BlockSpec((1, kb, w), lambda b, j: (b, j, 0))
    cblk = lambda w: pl.BlockSpec((1, kb, w), lambda b, j: (b + cache_off, j, 0))
    new = lambda w: pl.BlockSpec((1, rows_new, w), lambda b, j: (b, 0, 0))
    return pl.pallas_call(
        kern, grid=(bsz, past // kb),
        in_specs=[pl.BlockSpec((1, FOX_HEAD_COLS, LANES), lambda b, j: (b, 0, 0)),
                  cblk(FOX_HEAD_COLS), cblk(FOX_HEAD_COLS), blk(LANES),
                  new(FOX_HEAD_COLS), new(FOX_HEAD_COLS), new(LANES)],
        out_specs=pl.BlockSpec((1, n_new, FOX_HEAD_COLS), lambda b, j: (b, 0, 0)),
        out_shape=jax.ShapeDtypeStruct((bsz, n_new, FOX_HEAD_COLS), BF16),
        scratch_shapes=[pltpu.VMEM((1, LANES), F32), pltpu.VMEM((1, LANES), F32),
                        pltpu.VMEM((LANES, FOX_HEAD_COLS), F32), pltpu.VMEM((1, LANES), F32)],
        compiler_params=_cparams(("arbitrary", "arbitrary")), name="fox_sample",
    )(qbd, ck, cv, clf_exp, k_new, v_new, lf_new_exp)


def _layer_norm_rows(r, g, b):
    mu = jnp.mean(r, axis=1, keepdims=True)
    rc = r - mu
    var = jnp.mean(rc * rc, axis=1, keepdims=True)
    return rc * lax.rsqrt(var + LN_EPS) * g + b


def _outproj_router_kernel(hmp_ref, ofp_ref, xp_ref, hms_ref, ofs_ref, xs_ref,
                           wo_ref, lg_ref, lb_ref, wrh_ref, wrl_ref, rb_ref,
                           h1_ref, h1p_ref, eidx_ref, gw_ref, rank_ref, cnt_ref, cnt_s, *, ntp, n_valid):
    i = pl.program_id(0)
    tm = xp_ref.shape[0]

    @pl.when(i == 0)
    def _():
        cnt_s[...] = jnp.zeros(cnt_s.shape, F32)

    is_p = i < ntp
    hm = jnp.where(is_p, hmp_ref[...], hms_ref[...])
    of = jnp.where(is_p, ofp_ref[...], ofs_ref[...])
    x = jnp.where(is_p, xp_ref[...], xs_ref[...])
    half = hm.shape[1]
    mix = _dot(hm, wo_ref[0:half, :]) + _dot(of, wo_ref[half:, :])
    h1 = _layer_norm_rows(ALPHA * x + mix, lg_ref[...], lb_ref[...])
    h1_ref[...] = h1
    h1p_ref[...] = _pack_halves(h1)

    hh, hl = _split2(h1)
    logits = _dot_nt(wrh_ref[...], hh) + _dot_nt(wrl_ref[...], hh) + _dot_nt(wrh_ref[...], hl)
    scores = jax.nn.sigmoid(logits)
    biased = scores + rb_ref[...]

    neg = -jnp.inf
    sub = lax.broadcasted_iota(I32, (GROUP_SIZE, tm), 0).astype(F32)
    gs_rows = []
    for gi in range(N_GROUPS):
        grp = biased[gi * GROUP_SIZE:(gi + 1) * GROUP_SIZE, :]
        m1 = jnp.max(grp, axis=0, keepdims=True)
        i1 = jnp.min(jnp.where(grp == m1, sub, float(GROUP_SIZE)), axis=0, keepdims=True)
        m2 = jnp.max(jnp.where(sub == i1, neg, grp), axis=0, keepdims=True)
        gs_rows.append(m1 + m2)
    gsc = jnp.concatenate(gs_rows, axis=0)
    gid = lax.broadcasted_iota(I32, (N_GROUPS, tm), 0).astype(F32)
    gsel = jnp.zeros((N_GROUPS, tm), F32)
    for _ in range(TOP_GROUPS):
        m = jnp.max(gsc, axis=0, keepdims=True)
        idx = jnp.min(jnp.where(gsc == m, gid, float(N_GROUPS)), axis=0, keepdims=True)
        hit = gid == idx
        gsel = jnp.where(hit, 1.0, gsel)
        gsc = jnp.where(hit, neg, gsc)
    emask = jnp.concatenate(
        [jnp.broadcast_to(gsel[gi:gi + 1, :], (GROUP_SIZE, tm)) for gi in range(N_GROUPS)], axis=0)
    masked = jnp.where(emask > 0.0, biased, neg)

    eid = lax.broadcasted_iota(I32, (N_EXPERTS, tm), 0).astype(F32)
    sel = jnp.zeros((N_EXPERTS, tm), F32)
    idx_rows, gw_rows = [], []
    for _ in range(TOP_K):
        m = jnp.max(masked, axis=0, keepdims=True)
        idx = jnp.min(jnp.where(masked == m, eid, float(N_EXPERTS)), axis=0, keepdims=True)
        hit = eid == idx
        gw_rows.append(jnp.sum(jnp.where(hit, scores, 0.0), axis=0, keepdims=True))
        masked = jnp.where(hit, neg, masked)
        sel = jnp.where(hit, 1.0, sel)
        idx_rows.append(idx)
    tok = i * tm + lax.broadcasted_iota(I32, (1, tm), 1)
    sel = jnp.where(tok < n_valid, sel, 0.0)
    gsum = gw_rows[0]
    for r in gw_rows[1:]:
        gsum = gsum + r
    gscale = ROUTE_SCALE / gsum

    ri = lax.broadcasted_iota(I32, (tm, tm), 0)
    ci = lax.broadcasted_iota(I32, (tm, tm), 1)
    before = (ri < ci).astype(BF16)
    rankmat = _dot(sel.astype(BF16), before) + cnt_s[:, 0:1]
    cnt_s[...] = cnt_s[...] + jnp.sum(sel, axis=1, keepdims=True)
    for kk in range(TOP_K):
        hit = eid == idx_rows[kk]
        rk = jnp.sum(jnp.where(hit, rankmat, 0.0), axis=0, keepdims=True)
        eidx_ref[kk:kk + 1, :] = idx_rows[kk].astype(I32)
        gw_ref[kk:kk + 1, :] = gw_rows[kk] * gscale
        rank_ref[kk:kk + 1, :] = rk.astype(I32)

    @pl.when(i == pl.num_programs(0) - 1)
    def _():
        cnt_ref[...] = cnt_s[...]


def _outproj_router(hm_p, of_p, x_p, hm_s, of_s, x_s, w_out, ln_g, ln_b, wr_hi, wr_lo, rbias_col,
                    *, tm, n_valid):
    n_p, d = x_p.shape
    half = hm_p.shape[1]
    ntp = n_p // tm
    n_all = n_p + tm
    prow = lambda w: pl.BlockSpec((tm, w), lambda i: (jnp.minimum(i, ntp - 1), 0))
    srow = lambda w: pl.BlockSpec((tm, w), lambda i: (0, 0))
    orow = lambda w: pl.BlockSpec((tm, w), lambda i: (i, 0))
    trow = pl.BlockSpec((TOP_K, tm), lambda i: (0, i))
    cnt_spec = pl.BlockSpec((N_EXPERTS, LANES), lambda i: (0, 0))
    kern = functools.partial(_outproj_router_kernel, ntp=ntp, n_valid=n_valid)
    return pl.pallas_call(
        kern, grid=(ntp + 1,),
        in_specs=[prow(half), prow(half), prow(d), srow(half), srow(half), srow(d),
                  _const_spec(w_out.shape), _const_spec(ln_g.shape), _const_spec(ln_b.shape),
                  _const_spec(wr_hi.shape), _const_spec(wr_lo.shape), _const_spec(rbias_col.shape)],
        out_specs=[orow(d), orow(d // 2), trow, trow, trow, cnt_spec],
        out_shape=[jax.ShapeDtypeStruct((n_all, d), F32), jax.ShapeDtypeStruct((n_all, d // 2), U32),
                   jax.ShapeDtypeStruct((TOP_K, n_all), I32), jax.ShapeDtypeStruct((TOP_K, n_all), F32),
                   jax.ShapeDtypeStruct((TOP_K, n_all), I32), jax.ShapeDtypeStruct((N_EXPERTS, LANES), F32)],
        scratch_shapes=[pltpu.VMEM((N_EXPERTS, LANES), F32)],
        compiler_params=_cparams(("arbitrary",)), name="outproj_router",
    )(hm_p, of_p, x_p, hm_s, of_s, x_s, w_out, ln_g, ln_b, wr_hi, wr_lo, rbias_col)


def _dispatch_kernel(dest_ref, zflag_ref, h_ref, xs_hbm, zeros_s, sem, zsem, *, tm):
    i = pl.program_id(0)
    nblk = zflag_ref.shape[0]

    def zero_copy(b):
        rows = pl.ds(pl.multiple_of(b * EXPERT_BLOCK, EXPERT_BLOCK), EXPERT_BLOCK)
        return pltpu.make_async_copy(zeros_s, xs_hbm.at[rows, :], zsem)

    @pl.when(i == 0)
    def _():
        zeros_s[...] = jnp.zeros(zeros_s.shape, zeros_s.dtype)

        def start(b, _):
            @pl.when(zflag_ref[b] != 0)
            def _():
                zero_copy(b).start()
            return 0

        def wait(b, _):
            @pl.when(zflag_ref[b] != 0)
            def _():
                zero_copy(b).wait()
            return 0

        lax.fori_loop(0, nblk, start, 0)
        lax.fori_loop(0, nblk, wait, 0)

    def issue(t, _):
        for k in range(TOP_K):
            d = dest_ref[t * TOP_K + k]
            pltpu.make_async_copy(h_ref.at[pl.ds(t, 1), :], xs_hbm.at[pl.ds(d, 1), :], sem).start(priority=k % 2)
        return 0

    lax.fori_loop(0, tm, issue, 0)
    for _ in range(TOP_K):
        pltpu.make_async_copy(h_ref, xs_hbm.at[pl.ds(0, tm), :], sem).wait()


def _dispatch(h1p, dest_flat, zflag, *, tm, rows_alloc):
    n_all, c = h1p.shape
    kern = functools.partial(_dispatch_kernel, tm=tm)
    return pl.pallas_call(
        kern, grid=(n_all // tm,),
        in_specs=[pl.BlockSpec((tm * TOP_K,), lambda i: (i,), memory_space=pltpu.SMEM),
                  pl.BlockSpec(memory_space=pltpu.SMEM),
                  pl.BlockSpec((tm, c), lambda i: (i, 0))],
        out_specs=pl.BlockSpec(memory_space=pl.ANY),
        out_shape=jax.ShapeDtypeStruct((rows_alloc, c), U32),
        scratch_shapes=[pltpu.VMEM((EXPERT_BLOCK, c), U32), pltpu.SemaphoreType.DMA(()),
                        pltpu.SemaphoreType.DMA(())],
        compiler_params=_cparams(("arbitrary",)), name="dispatch",
    )(dest_flat, zflag, h1p)


def _expert_kernel(blk_e_ref, nact_ref, next_e_ref, xs_ref, wg_hbm, wu_hbm, wd_hbm, ys_ref,
                   wg32, wu32, wd32, wgu_b, wd_b, sems, slot_ref):
    b = pl.program_id(0)
    e = blk_e_ref[b]
    de = wg32.shape[2]

    def weight_copies(expert, slot):
        return (pltpu.make_async_copy(wg_hbm.at[expert], wg32.at[slot], sems.at[slot, 0]),
                pltpu.make_async_copy(wu_hbm.at[expert], wu32.at[slot], sems.at[slot, 1]),
                pltpu.make_async_copy(wd_hbm.at[expert], wd32.at[slot], sems.at[slot, 2]))

    @pl.when(b == 0)
    def _():
        slot_ref[0] = 0
        for c in weight_copies(e, 0):
            c.start()

    active = b < nact_ref[0]
    first_of_expert = jnp.logical_or(b == 0, blk_e_ref[jnp.maximum(b - 1, 0)] != e)

    @pl.when(jnp.logical_and(active, first_of_expert))
    def _():
        slot = slot_ref[0]
        for c in weight_copies(e, slot):
            c.wait()
        wgu_b[:, 0:de] = wg32[slot].astype(BF16)
        wgu_b[:, de:2 * de] = wu32[slot].astype(BF16)
        wd_b[...] = wd32[slot].astype(BF16)
        nxt = next_e_ref[b]

        @pl.when(nxt >= 0)
        def _():
            for c in weight_copies(nxt, 1 - slot):
                c.start(priority=1)

        slot_ref[0] = 1 - slot

    @pl.when(active)
    def _():
        x = _unpack_halves(xs_ref[...]).astype(BF16)
        h = _dot(x, wgu_b[...])
        hb = (_silu(h[:, 0:de]) * h[:, de:2 * de]).astype(BF16)
        ys_ref[...] = _pack_halves(_dot(hb, wd_b[...]))

    @pl.when(jnp.logical_not(active))
    def _():
        ys_ref[...] = jnp.zeros(ys_ref.shape, ys_ref.dtype)


def _experts(xs, w_gate, w_up, w_down, blk_e, n_active, next_e):
    n_exp, d, de = w_gate.shape
    nb = blk_e.shape[0]
    rows = nb * EXPERT_BLOCK

    def live(b, nact):
        return jnp.minimum(b, jnp.maximum(nact[0] - 1, 0))

    hbm = pl.BlockSpec(memory_space=pl.ANY)
    grid_spec = pltpu.PrefetchScalarGridSpec(
        num_scalar_prefetch=3, grid=(nb,),
        in_specs=[pl.BlockSpec((EXPERT_BLOCK, d // 2), lambda b, be, na, ne: (live(b, na), 0)), hbm, hbm, hbm],
        out_specs=pl.BlockSpec((EXPERT_BLOCK, d // 2), lambda b, be, na, ne: (b, 0)),
        scratch_shapes=[pltpu.VMEM((2, d, de), F32), pltpu.VMEM((2, d, de), F32), pltpu.VMEM((2, de, d), F32),
                        pltpu.VMEM((d, 2 * de), BF16), pltpu.VMEM((de, d), BF16),
                        pltpu.SemaphoreType.DMA((2, 3)), pltpu.SMEM((1,), I32)])
    return pl.pallas_call(
        _expert_kernel, grid_spec=grid_spec,
        out_shape=jax.ShapeDtypeStruct((rows, d // 2), U32),
        compiler_params=_cparams(("arbitrary",)), name="experts",
    )(blk_e, n_active, next_e, xs, w_gate, w_up, w_down)


def _combine_ln_kernel(src_ref, h1_ref, h1p_ref, gw_ref, ys_hbm, wg_ref, wu_ref, wd_ref, lg_ref, lb_ref,
                       yp_ref, ysm_ref, g_s, sem, *, ntp):
    i = pl.program_id(0)
    tm = h1_ref.shape[0]

    def issue(t, _):
        for k in range(TOP_K):
            r = src_ref[t * TOP_K + k]
            pltpu.make_async_copy(ys_hbm.at[pl.ds(r, 1), :], g_s.at[k, pl.ds(t, 1), :], sem).start(priority=k % 2)
        return 0

    lax.fori_loop(0, tm, issue, 0)
    xb = _unpack_halves(h1p_ref[...]).astype(BF16)
    hb = (_silu(_dot(xb, wg_ref[...])) * _dot(xb, wu_ref[...])).astype(BF16)
    f = _dot(hb, wd_ref[...])
    for k in range(TOP_K):
        pltpu.make_async_copy(ys_hbm.at[pl.ds(0, tm), :], g_s.at[k], sem).wait()
    gw = gw_ref[...]
    for k in range(TOP_K):
        f = f + gw[:, k:k + 1] * _unpack_halves(g_s[k])
    y = _layer_norm_rows(ALPHA * h1_ref[...] + f, lg_ref[...], lb_ref[...])

    @pl.when(i < ntp)
    def _():
        yp_ref[...] = y

    @pl.when(i == ntp)
    def _():
        ysm_ref[...] = y


def _combine_ln(src_flat, h1, h1p, gw, ys, ws_gate, ws_up, ws_down, ln_g, ln_b, *, tm, n_p):
    n_all, d = h1.shape
    ntp = n_p // tm
    kern = functools.partial(_combine_ln_kernel, ntp=ntp)
    return pl.pallas_call(
        kern, grid=(n_all // tm,),
        in_specs=[pl.BlockSpec((tm * TOP_K,), lambda i: (i,), memory_space=pltpu.SMEM),
                  pl.BlockSpec((tm, d), lambda i: (i, 0)), pl.BlockSpec((tm, d // 2), lambda i: (i, 0)),
                  pl.BlockSpec((tm, TOP_K), lambda i: (i, 0)), pl.BlockSpec(memory_space=pl.ANY),
                  _const_spec(ws_gate.shape), _const_spec(ws_up.shape),
                  _const_spec(ws_down.shape), _const_spec(ln_g.shape), _const_spec(ln_b.shape)],
        out_specs=[pl.BlockSpec((tm, d), lambda i: (jnp.minimum(i, ntp - 1), 0)),
                   pl.BlockSpec((tm, d), lambda i: (0, 0))],
        out_shape=[jax.ShapeDtypeStruct((ntp * tm, d), F32), jax.ShapeDtypeStruct((tm, d), F32)],
        scratch_shapes=[pltpu.VMEM((TOP_K, tm, d // 2), U32), pltpu.SemaphoreType.DMA(())],
        compiler_params=_cparams(("arbitrary",)), name="combine_ln",
    )(src_flat, h1, h1p, gw, ys, ws_gate, ws_up, ws_down, ln_g, ln_b)


def _prep_weights(w_in, b_gate, conv_w, conv_b, norm_g, w_out, ln1_g, ln1_b, w_router, router_bias,
                  w_s_gate, w_s_up, w_s_down, ln2_g, ln2_b):
    d = w_in.shape[0]
    n_main = QK_COLS + 2 * VM_COLS + 3 * FOX_HEAD_COLS
    n_gate = w_in.shape[1] - n_main
    w_main = w_in[:, :n_main].astype(BF16)
    wg = jnp.zeros((d, LANES), F32).at[:, :n_gate].set(w_in[:, n_main:])
    bg = jnp.zeros((LANES,), F32).at[:n_gate].set(b_gate)
    for c in FOX_GATE_COPIES[1:]:
        wg = wg.at[:, c:c + HF].set(w_in[:, n_main + 2 * HM:])
        bg = bg.at[c:c + HF].set(b_gate[2 * HM:])
    wg = wg.astype(BF16)
    wr_hi = w_router.T.astype(BF16)
    wr_lo = (w_router.T - wr_hi.astype(F32)).astype(BF16)
    return dict(
        w_main=w_main, w_gate=wg, w_gate_t=wg.T, bg_row=bg[None, :], bg_col=bg[:, None],
        conv_w8=jnp.zeros((SUBLANES, QK_COLS), F32).at[:CONV_W].set(conv_w), conv_b=conv_b[None, :],
        norm_g=norm_g[None, :], w_out=w_out.astype(BF16), ln1_g=ln1_g[None, :], ln1_b=ln1_b[None, :],
        wr_hi=wr_hi, wr_lo=wr_lo, rbias=router_bias[:, None],
        ws_gate=w_s_gate.astype(BF16), ws_up=w_s_up.astype(BF16), ws_down=w_s_down.astype(BF16),
        ln2_g=ln2_g[None, :], ln2_b=ln2_b[None, :])


def _pad_rows(a, bsz, t, tp, value=0.0):
    w = a.shape[-1]
    a = a.reshape(bsz, t, w)
    pad = jnp.full((bsz, tp - t, w), value, a.dtype) if not hasattr(value, "shape") else \
        jnp.broadcast_to(value.astype(a.dtype), (bsz, tp - t, w))
    return jnp.concatenate([a, pad], axis=1).reshape(bsz * tp, w)


def _mixer_prompt(x, wts, *, tm, chunk, tb):
    bsz, t, d = x.shape
    n = bsz * t
    x2 = x.reshape(n, d)
    uqk, vm, om, qt, kf, vf, kb, vt, g, gt = _inproj(
        x2, wts["w_main"], wts["w_gate"], wts["w_gate_t"], wts["bg_row"], wts["bg_col"],
        tm=tm, seq_len=t, prompt=True)
    fcols, lf = _fox_cumsum(g, bsz, t, min(512, t))
    zeros = lambda *s: jnp.zeros(s, F32)
    hm, c1, n1, m1 = _mlstm(uqk, vm, om, g, gt, zeros(bsz, SUBLANES, QK_COLS), wts["conv_w8"], wts["conv_b"],
                            wts["norm_g"], zeros(bsz, HM, DQK, DV), zeros(bsz, HM, 1, DQK),
                            zeros(bsz, HM, 1, LANES), bsz=bsz, seq_len=t, chunk=chunk)
    of = _fox_prompt(qt, kb, vt, fcols, tb=tb).reshape(n, FOX_HEAD_COLS)
    state = (kf.reshape(bsz, t, HF, DHF), vf.reshape(bsz, t, HF, DHF),
             lf[:, 2 * HM:2 * HM + HF].reshape(bsz, t, HF), c1, n1[:, :, 0, :], m1[:, :, 0, 0],
             uqk.reshape(bsz, t, QK_COLS)[:, t - (CONV_W - 1):, :])
    return x2, hm, of, state


def _mixer_sample(x, conv_hist, c0, n0, m0, ck, cv, clogf, wts, *, chunk, kb, cache_off):
    bsz, t, d = x.shape
    n = bsz * t
    past = ck.shape[1]
    x2 = x.reshape(n, d)
    uqk, vm, om, qf, kf, vf, g, gt = _inproj(
        x2, wts["w_main"], wts["w_gate"], wts["w_gate_t"], wts["bg_row"], wts["bg_col"],
        tm=n, seq_len=t, prompt=False)
    gpad = jnp.concatenate([jnp.full((HM,), NEG_BIG, F32), jnp.full((HM,), -NEG_BIG, F32),
                            jnp.zeros((LANES - 2 * HM,), F32)])
    g_p = _pad_rows(g, bsz, t, chunk, gpad)
    hist8 = jnp.concatenate([jnp.zeros((bsz, SUBLANES - (CONV_W - 1), QK_COLS), F32), conv_hist], axis=1)
    hm_p, c1, n1, m1 = _mlstm(
        _pad_rows(uqk, bsz, t, chunk), _pad_rows(vm, bsz, t, chunk), _pad_rows(om, bsz, t, chunk),
        g_p, g_p.T, hist8, wts["conv_w8"], wts["conv_b"], wts["norm_g"],
        c0, n0[:, :, None, :], jnp.broadcast_to(m0[:, :, None, None], (bsz, HM, 1, LANES)),
        bsz=bsz, seq_len=chunk, chunk=chunk)
    hm = hm_p.reshape(bsz, chunk, VM_COLS)[:, :t].reshape(n, VM_COLS)
    lf_new = _log_sigmoid_rows(g[:, 2 * HM:2 * HM + HF])
    q4 = qf.reshape(bsz, t, HF, DHF)
    eye = jnp.eye(HF, dtype=BF16)
    qbd = jnp.einsum("bthd,hg->bhdgt", q4, eye).reshape(bsz, FOX_HEAD_COLS, HF * t)
    rows_new = LANES
    expand = lambda a: jnp.repeat(a, t, axis=-1)
    pad3 = lambda a: jnp.concatenate(
        [a, jnp.zeros((bsz, rows_new - t, a.shape[-1]), a.dtype)], axis=1)
    of = _fox_sample(qbd, ck.reshape(-1, past, FOX_HEAD_COLS), cv.reshape(-1, past, FOX_HEAD_COLS),
                     expand(clogf), pad3(kf.reshape(bsz, t, FOX_HEAD_COLS)),
                     pad3(vf.reshape(bsz, t, FOX_HEAD_COLS)), pad3(expand(lf_new.reshape(bsz, t, HF))),
                     kb=kb, n_new=t, cache_off=cache_off).reshape(n, FOX_HEAD_COLS)
    state = (kf.reshape(bsz, t, HF, DHF), vf.reshape(bsz, t, HF, DHF), lf_new.reshape(bsz, t, HF),
             c1, n1[:, :, 0, :], m1[:, :, 0, 0],
             jnp.concatenate([conv_hist, uqk.reshape(bsz, t, QK_COLS)], axis=1)[:, -(CONV_W - 1):, :])
    return x2, hm, of, state


def _log_sigmoid_rows(a):
    n, w = a.shape
    ap = jnp.zeros((n, LANES), F32).at[:, :w].set(a)

    def kern(a_ref, o_ref):
        o_ref[...] = _log_sigmoid(a_ref[...])

    out = pl.pallas_call(kern, out_shape=jax.ShapeDtypeStruct((n, LANES), F32), name="log_sigmoid")(ap)
    return out[:, :w]


def _dest_rows_kernel(pstart_ref, eidx_ref, rank_ref, o_ref):
    e = eidx_ref[...]

    def add_expert(x, acc):
        return acc + jnp.where(e == x, pstart_ref[x], 0)

    o_ref[...] = lax.fori_loop(0, N_EXPERTS, add_expert, rank_ref[...])


def _dest_rows(pstart, eidx_t, rank_t):
    k, n = eidx_t.shape
    tn = max(c for c in range(LANES, 8192 + 1, LANES) if n % c == 0)
    spec = pl.BlockSpec((k, tn), lambda i: (0, i))
    return pl.pallas_call(
        _dest_rows_kernel, grid=(n // tn,),
        in_specs=[pl.BlockSpec(memory_space=pltpu.SMEM), spec, spec], out_specs=spec,
        out_shape=jax.ShapeDtypeStruct((k, n), I32),
        compiler_params=_cparams(("arbitrary",)), name="dest_rows",
    )(pstart, eidx_t, rank_t)


def _dispatch_tables(counts, n_rows_total):
    blk = EXPERT_BLOCK
    pcounts = (counts + blk - 1) // blk * blk
    pend = jnp.cumsum(pcounts)
    pstart = pend - pcounts
    nb = n_rows_total // blk
    idx = jnp.arange(nb, dtype=I32)
    blk_e = jnp.minimum(jnp.sum(pend[None, :] <= (idx * blk)[:, None], axis=1), N_EXPERTS - 1).astype(I32)
    n_active = (pend[-1] // blk).astype(I32)
    starts = jnp.concatenate([jnp.ones((1,), bool), blk_e[1:] != blk_e[:-1]])
    next_start = lax.cummin(jnp.where(starts, idx, nb), reverse=True)
    next_start = jnp.concatenate([next_start[1:], jnp.full((1,), nb, I32)])
    next_e = jnp.where(next_start < n_active, blk_e[jnp.minimum(next_start, nb - 1)], -1).astype(I32)
    last_of_run = jnp.concatenate([starts[1:], jnp.ones((1,), bool)])
    zflag = ((idx >= n_active - 1) | last_of_run).astype(I32)
    return pstart.astype(I32), blk_e, n_active.reshape(1), next_e, zflag


def kernel(x_prompt, x_sample, cache_fox_k, cache_fox_v, cache_fox_logf, state_mlstm_C, state_mlstm_n,
           state_mlstm_m, state_conv, w_in, b_gate, conv_w, conv_b, mlstm_norm_g, w_out, ln1_g, ln1_b,
           w_router, router_bias, w_e_gate, w_e_up, w_e_down, w_s_gate, w_s_up, w_s_down, ln2_g, ln2_b):
    l = 0
    bp, tp, d = x_prompt.shape
    bs, ts, _ = x_sample.shape
    n_p, n_s = bp * tp, bs * ts
    n_tot = n_p + n_s
    wts = _prep_weights(w_in[l], b_gate[l], conv_w[l], conv_b[l], mlstm_norm_g[l], w_out[l], ln1_g[l], ln1_b[l],
                        w_router[l], router_bias[l], w_s_gate[l], w_s_up[l], w_s_down[l], ln2_g[l], ln2_b[l])

    tm_p = min(256, tp)
    xp2, hm_p, of_p, st_p = _mixer_prompt(x_prompt, wts, tm=tm_p, chunk=min(256, tp), tb=min(512, tp))
    xs2, hm_s, of_s, st_s = _mixer_sample(
        x_sample, state_conv[l], state_mlstm_C[l], state_mlstm_n[l], state_mlstm_m[l],
        cache_fox_k.reshape((-1,) + cache_fox_k.shape[2:]), cache_fox_v.reshape((-1,) + cache_fox_v.shape[2:]),
        cache_fox_logf[l], wts, chunk=LANES, kb=min(512, cache_fox_k.shape[2]), cache_off=l * bs)

    tm = tm_p
    tm_r = min(ROUTER_TILE, tp)
    padr = lambda a: jnp.concatenate([a, jnp.zeros((tm_r - n_s, a.shape[1]), a.dtype)], axis=0)
    h1, h1p, eidx_t, gw_t, rank_t, cnt = _outproj_router(
        hm_p, of_p, xp2, padr(hm_s), padr(of_s), padr(xs2),
        wts["w_out"], wts["ln1_g"], wts["ln1_b"], wts["wr_hi"], wts["wr_lo"], wts["rbias"],
        tm=tm_r, n_valid=n_tot)
    n_all = n_p + tm_r
    n_pad = n_all - n_tot
    counts = cnt[:, 0].astype(I32)

    nk = n_tot * TOP_K
    n_blocks = min(N_EXPERTS, nk) + nk // EXPERT_BLOCK
    rows_total = n_blocks * EXPERT_BLOCK
    pstart, blk_e, n_active, next_e, zflag = _dispatch_tables(counts, rows_total)
    dest = _dest_rows(pstart, eidx_t, rank_t).T
    tok = jnp.arange(n_all, dtype=I32)[:, None]
    park = rows_total + (tok - n_tot) * TOP_K + jnp.arange(TOP_K, dtype=I32)[None, :]
    real = tok < n_tot
    xs = _dispatch(h1p, jnp.where(real, dest, park).reshape(-1), zflag,
                   tm=tm, rows_alloc=rows_total + n_pad * TOP_K)
    de = w_e_gate.shape[-1]
    ys = _experts(xs, w_e_gate.reshape(-1, d, de), w_e_up.reshape(-1, d, de), w_e_down.reshape(-1, de, d),
                  blk_e + l * N_EXPERTS, n_active, jnp.where(next_e >= 0, next_e + l * N_EXPERTS, -1))
    src = jnp.where(real, dest, 0)
    gw_all = jnp.where(real, gw_t.T, 0.0)
    y_p, y_s = _combine_ln(src.reshape(-1), h1, h1p, gw_all, ys, wts["ws_gate"], wts["ws_up"], wts["ws_down"],
                           wts["ln2_g"], wts["ln2_b"], tm=tm, n_p=n_p)
    y_p = y_p.reshape(bp, tp, d)
    y_s = y_s[:n_s].reshape(bs, ts, d)

    stack = lambda s: tuple(a[None] for a in s)
    return (y_p, y_s) + stack(st_p) + stack(st_s)
```

```python
import functools
import math

import jax
import jax.numpy as jnp
from jax import lax
from jax.experimental import pallas as pl
from jax.experimental.pallas import tpu as pltpu

F32 = jnp.float32
BF16 = jnp.bfloat16
I32 = jnp.int32
U32 = jnp.uint32

HM, DQK, DV = 4, 128, 256
HF, DHF = 8, 128
CONV_W = 4
QK_COLS = 2 * HM * DQK
VM_COLS = HM * DV
FOX_HEAD_COLS = HF * DHF
N_EXPERTS, TOP_K, N_GROUPS, TOP_GROUPS = 256, 8, 8, 4
GROUP_SIZE = N_EXPERTS // N_GROUPS
ROUTE_SCALE = 2.5
DEPTH = 1
ALPHA = (2 * DEPTH) ** 0.25
LN_EPS = 1e-5
HEAD_EPS = 1e-6
LOG2E = math.log2(math.e)

LANES = 128
SUBLANES = 8
VMEM_LIMIT_BYTES = 56 * 1024 * 1024

NEG_BIG = -1e30
EXPERT_BLOCK = 256
ROUTER_TILE = 512


def _dot(a, b):
    return jnp.dot(a, b, preferred_element_type=F32)


def _dot_nt(a, b):
    return lax.dot_general(a, b, (((1,), (1,)), ((), ())), preferred_element_type=F32)


def _dot_tn(a, b):
    return lax.dot_general(a, b, (((0,), (0,)), ((), ())), preferred_element_type=F32)


def _split3(x):
    hi = x.astype(BF16)
    r1 = x - hi.astype(F32)
    mid = r1.astype(BF16)
    lo = (r1 - mid.astype(F32)).astype(BF16)
    return hi, mid, lo


def _split2(x):
    hi = x.astype(BF16)
    lo = (x - hi.astype(F32)).astype(BF16)
    return hi, lo


def _pack_halves(x):
    c = x.shape[1] // 2
    lo = lax.bitcast_convert_type(x[:, :c].astype(BF16).astype(F32), U32)
    hi = lax.bitcast_convert_type(x[:, c:].astype(BF16).astype(F32), U32)
    return (lo >> 16) | hi


def _unpack_halves(p):
    lo = lax.bitcast_convert_type(p << 16, F32)
    hi = lax.bitcast_convert_type(p & jnp.uint32(0xFFFF0000), F32)
    return jnp.concatenate([lo, hi], axis=1)


def _log_sigmoid(x):
    return jnp.minimum(x, 0.0) - jnp.log1p(jnp.exp(-jnp.abs(x)))


def _silu(x):
    return x * jax.nn.sigmoid(x)


def _cparams(sem):
    return pltpu.CompilerParams(dimension_semantics=sem, vmem_limit_bytes=VMEM_LIMIT_BYTES)


def _const_spec(shape):
    nd = len(shape)
    return pl.BlockSpec(shape, lambda *_: (0,) * nd, pipeline_mode=pl.Buffered(1))


def _inproj_kernel(x_ref, w_ref, wg_ref, wgt_ref, bg_ref, bgt_ref, *outs, prompt, q_scale):
    xb = x_ref[...].astype(BF16)
    if prompt:
        uqk_ref, vm_ref, om_ref, qf_ref, kf_ref, vf_ref, kb_ref, vt_ref, g_ref, gt_ref = outs
    else:
        uqk_ref, vm_ref, om_ref, qf_ref, kf_ref, vf_ref, g_ref, gt_ref = outs
    o = 0
    uqk_ref[...] = _dot(xb, w_ref[:, o:o + QK_COLS])
    o += QK_COLS
    vm_ref[...] = _dot(xb, w_ref[:, o:o + VM_COLS]).astype(BF16)
    o += VM_COLS
    om_ref[...] = jax.nn.sigmoid(_dot(xb, w_ref[:, o:o + VM_COLS])).astype(BF16)
    o += VM_COLS
    q = _dot(xb, w_ref[:, o:o + FOX_HEAD_COLS]) * q_scale
    o += FOX_HEAD_COLS
    k = _dot(xb, w_ref[:, o:o + FOX_HEAD_COLS])
    o += FOX_HEAD_COLS
    v = _dot(xb, w_ref[:, o:o + FOX_HEAD_COLS])
    kf_ref[...] = k
    vf_ref[...] = v
    if prompt:
        qf_ref[0] = q.T.astype(BF16)
        vt_ref[0] = v.T.astype(BF16)
        for h in range(HF):
            kb_ref[0, h] = k[:, h * DHF:(h + 1) * DHF].astype(BF16)
    else:
        qf_ref[...] = q.astype(BF16)
    g_ref[...] = _dot(xb, wg_ref[...]) + bg_ref[...]
    gt_ref[...] = _dot_nt(wgt_ref[...], xb) + bgt_ref[...]


def _inproj(x2, w_main, w_gate, w_gate_t, b_gate_row, b_gate_col, *, tm, seq_len, prompt):
    n, d = x2.shape
    nt = n // tm
    grid = (nt,)
    row = lambda w: pl.BlockSpec((tm, w), lambda i: (i, 0))
    out_shape = [jax.ShapeDtypeStruct((n, QK_COLS), F32),
                 jax.ShapeDtypeStruct((n, VM_COLS), BF16),
                 jax.ShapeDtypeStruct((n, VM_COLS), BF16)]
    out_specs = [row(QK_COLS), row(VM_COLS), row(VM_COLS)]
    f32_rows = jax.ShapeDtypeStruct((n, FOX_HEAD_COLS), F32)
    if prompt:
        bsz = n // seq_len
        tpb = seq_len // tm
        t_shape = jax.ShapeDtypeStruct((bsz, FOX_HEAD_COLS, seq_len), BF16)
        t_spec = pl.BlockSpec((1, FOX_HEAD_COLS, tm), lambda i: (i // tpb, 0, i % tpb))
        out_shape += [t_shape, f32_rows, f32_rows,
                      jax.ShapeDtypeStruct((bsz, HF, seq_len, DHF), BF16), t_shape]
        out_specs += [t_spec, row(FOX_HEAD_COLS), row(FOX_HEAD_COLS),
                      pl.BlockSpec((1, HF, tm, DHF), lambda i: (i // tpb, 0, i % tpb, 0)), t_spec]
    else:
        out_shape += [jax.ShapeDtypeStruct((n, FOX_HEAD_COLS), BF16), f32_rows, f32_rows]
        out_specs += [row(FOX_HEAD_COLS), row(FOX_HEAD_COLS), row(FOX_HEAD_COLS)]
    out_shape += [jax.ShapeDtypeStruct((n, LANES), F32), jax.ShapeDtypeStruct((LANES, n), F32)]
    out_specs += [row(LANES), pl.BlockSpec((LANES, tm), lambda i: (0, i))]
    kern = functools.partial(_inproj_kernel, prompt=prompt, q_scale=DHF ** -0.5 * LOG2E)
    return pl.pallas_call(
        kern, grid=grid,
        in_specs=[row(d), _const_spec(w_main.shape), _const_spec(w_gate.shape),
                  _const_spec(w_gate_t.shape), _const_spec(b_gate_row.shape), _const_spec(b_gate_col.shape)],
        out_specs=out_specs, out_shape=out_shape,
        compiler_params=_cparams(("arbitrary",)), name="inproj",
    )(x2, w_main, w_gate, w_gate_t, b_gate_row, b_gate_col)


def _fox_cumsum_kernel(g_ref, fc_ref, lf_ref, carry_s):
    @pl.when(pl.program_id(1) == 0)
    def _():
        carry_s[...] = jnp.zeros(carry_s.shape, F32)

    lf = _log_sigmoid(g_ref[...])
    lf_ref[...] = lf
    r = lf.shape[0]
    tril = (lax.broadcasted_iota(I32, (r, r), 1) <= lax.broadcasted_iota(I32, (r, r), 0)).astype(BF16)
    cs = carry_s[...] + sum(_dot(tril, p) for p in _split3(lf))
    carry_s[...] = cs[r - 1:r, :]
    fc_ref[...] = cs * LOG2E


def _fox_cumsum(g, bsz, seq_len, rows):
    nc = seq_len // rows
    spec = pl.BlockSpec((rows, LANES), lambda b, c: (b * nc + c, 0))
    shape = jax.ShapeDtypeStruct((bsz * seq_len, LANES), F32)
    return pl.pallas_call(
        _fox_cumsum_kernel, grid=(bsz, nc),
        in_specs=[spec], out_specs=[spec, spec], out_shape=[shape, shape],
        scratch_shapes=[pltpu.VMEM((1, LANES), F32)],
        compiler_params=_cparams(("arbitrary", "arbitrary")), name="fox_cumsum",
    )(g)


def _mlstm_kernel(uqk_ref, v_ref, om_ref, g_ref, gt_ref, hist_ref, cw_ref, cb_ref, ng_ref,
                  c0_ref, n0_ref, m0_ref, h_ref, c1_ref, n1_ref, m1_ref,
                  c_s, n_s, m_s, prev_s):
    c_idx = pl.program_id(1)
    nc = pl.num_programs(1)
    L = uqk_ref.shape[0]

    @pl.when(c_idx == 0)
    def _():
        c_s[...] = c0_ref[0]
        n_s[...] = n0_ref[0]
        m_s[...] = m0_ref[0]
        prev_s[...] = hist_ref[0]

    z = uqk_ref[...]
    ext = jnp.concatenate([prev_s[...], z], axis=0)
    prev_s[...] = z[L - SUBLANES:, :]
    cw = cw_ref[...]
    acc = cb_ref[...] + ext[SUBLANES:SUBLANES + L] * cw[CONV_W - 1:CONV_W]
    for j in range(CONV_W - 1):
        off = SUBLANES - (CONV_W - 1) + j
        acc = acc + ext[off:off + L] * cw[j:j + 1]
    qk = _silu(acc)

    g = g_ref[...]
    gt = gt_ref[...]
    lsg = _log_sigmoid(g)
    lsgt = _log_sigmoid(gt)
    ri = lax.broadcasted_iota(I32, (L, L), 0)
    ci = lax.broadcasted_iota(I32, (L, L), 1)
    causal = ci <= ri
    tril = causal.astype(BF16)
    triu = (ri <= ci).astype(BF16)
    b_cols = sum(_dot(tril, p) for p in _split3(lsg))
    b_rows = sum(_dot(p, triu) for p in _split3(lsgt))

    for h in range(HM):
        q = qk[:, h * DQK:(h + 1) * DQK]
        k = qk[:, HM * DQK + h * DQK:HM * DQK + (h + 1) * DQK] * (DQK ** -0.5)
        qb = q.astype(BF16)
        kb = k.astype(BF16)
        v = v_ref[:, h * DV:(h + 1) * DV]
        b_col = b_cols[:, HM + h:HM + h + 1]
        b_row = b_rows[HM + h:HM + h + 1, :]
        li_col = g[:, h:h + 1]
        li_row = gt[h:h + 1, :]
        m_prev = m_s[h][:, 0:1]
        c_prev = c_s[h]
        n_prev = n_s[h]

        c_row = li_row - b_row
        dmat = jnp.where(causal, b_col + c_row, -jnp.inf)
        inter = b_col + m_prev
        m_t = jnp.maximum(inter, jnp.max(dmat, axis=1, keepdims=True))
        w_intra = jnp.exp(dmat - m_t)
        w_state = jnp.exp(inter - m_t)
        s = _dot_nt(qb, kb) * w_intra
        num = _dot(s.astype(BF16), v) + w_state * _dot(qb, c_prev.astype(BF16))
        den = jnp.sum(s, axis=1, keepdims=True) + w_state * jnp.sum(q * n_prev, axis=1, keepdims=True)
        hh = num / jnp.maximum(jnp.abs(den), jnp.exp(-m_t))

        b_end = b_col[L - 1:L, :]
        a_col = b_end - b_col + li_col
        a_row = b_end + c_row
        m_new = jnp.maximum(b_end + m_prev, jnp.max(a_row, axis=1, keepdims=True))
        decay = jnp.exp(b_end + m_prev - m_new)
        wk = jnp.exp(a_col - m_new) * k
        c_s[h] = decay * c_prev + _dot_tn(wk.astype(BF16), v)
        n_s[h] = decay * n_prev + jnp.sum(wk, axis=0, keepdims=True)
        m_s[h] = jnp.broadcast_to(m_new, (1, LANES))

        mu = jnp.mean(hh, axis=1, keepdims=True)
        hc = hh - mu
        var = jnp.mean(hc * hc, axis=1, keepdims=True)
        hn = hc * lax.rsqrt(var + HEAD_EPS)
        sl = slice(h * DV, (h + 1) * DV)
        h_ref[:, sl] = (om_ref[:, sl].astype(F32) * hn * ng_ref[:, sl]).astype(BF16)

    @pl.when(c_idx == nc - 1)
    def _():
        c1_ref[0] = c_s[...]
        n1_ref[0] = n_s[...]
        m1_ref[0] = m_s[...]


def _mlstm(uqk, vm, om, g, g_t, hist8, conv_w8, conv_b, norm_g, c0, n0, m0, *, bsz, seq_len, chunk):
    nc = seq_len // chunk
    rowspec = lambda w: pl.BlockSpec((chunk, w), lambda b, c: (b * nc + c, 0))
    st = lambda shape: pl.BlockSpec((1,) + shape, lambda b, c: (b,) + (0,) * len(shape))
    return pl.pallas_call(
        _mlstm_kernel, grid=(bsz, nc),
        in_specs=[rowspec(QK_COLS), rowspec(VM_COLS), rowspec(VM_COLS), rowspec(LANES),
                  pl.BlockSpec((SUBLANES, chunk), lambda b, c: (0, b * nc + c)),
                  st((SUBLANES, QK_COLS)),
                  pl.BlockSpec((SUBLANES, QK_COLS), lambda b, c: (0, 0)),
                  pl.BlockSpec((1, QK_COLS), lambda b, c: (0, 0)),
                  pl.BlockSpec((1, VM_COLS), lambda b, c: (0, 0)),
                  st((HM, DQK, DV)), st((HM, 1, DQK)), st((HM, 1, LANES))],
        out_specs=[rowspec(VM_COLS), st((HM, DQK, DV)), st((HM, 1, DQK)), st((HM, 1, LANES))],
        out_shape=[jax.ShapeDtypeStruct((bsz * seq_len, VM_COLS), BF16),
                   jax.ShapeDtypeStruct((bsz, HM, DQK, DV), F32),
                   jax.ShapeDtypeStruct((bsz, HM, 1, DQK), F32),
                   jax.ShapeDtypeStruct((bsz, HM, 1, LANES), F32)],
        scratch_shapes=[pltpu.VMEM((HM, DQK, DV), F32), pltpu.VMEM((HM, 1, DQK), F32),
                        pltpu.VMEM((HM, 1, LANES), F32), pltpu.VMEM((SUBLANES, QK_COLS), F32)],
        compiler_params=_cparams(("arbitrary", "arbitrary")), name="mlstm",
    )(uqk, vm, om, g, g_t, hist8, conv_w8, conv_b, norm_g, c0, n0, m0)


FOX_GATE_COPIES = (8, 24, 40)


def _fox_prompt_kernel(qt_ref, k_ref, vt_ref, fc_ref, o_ref, kaug_s, u_s, p_s, acc_s, st_s, *, tb, prep_rows):
    h = pl.program_id(1)
    i = pl.program_id(2)
    seq = k_ref.shape[2]

    @pl.when(i == 0)
    def _():
        lane = lax.broadcasted_iota(I32, (prep_rows, LANES), 1)
        zero = jnp.zeros((prep_rows, LANES), BF16)

        def prep(c, _):
            rs = pl.ds(pl.multiple_of(c * prep_rows, prep_rows), prep_rows)
            hi, mid, lo = _split3(-fc_ref[rs, :])
            aug = jnp.where(lane == FOX_GATE_COPIES[0] + h, hi,
                            jnp.where(lane == FOX_GATE_COPIES[1] + h, mid,
                                      jnp.where(lane == FOX_GATE_COPIES[2] + h, lo, zero)))
            kaug_s[rs, 0:DHF] = k_ref[0, 0, rs, :]
            kaug_s[rs, DHF:2 * DHF] = aug
            return 0

        lax.fori_loop(0, seq // prep_rows, prep, 0)

    row = lax.broadcasted_iota(I32, (DHF, tb), 0)
    ones = jnp.where(row == FOX_GATE_COPIES[0] + h, 1.0,
                     jnp.where(row == FOX_GATE_COPIES[1] + h, 1.0,
                               jnp.where(row == FOX_GATE_COPIES[2] + h, 1.0, 0.0))).astype(BF16)
    qaug = jnp.concatenate([qt_ref[0], ones], axis=0)

    def key_rows(r):
        return pl.ds(pl.multiple_of(jnp.maximum(i - r, 0) * tb, tb), tb)

    def scores(r, slot):
        u_s[slot] = _dot(kaug_s[key_rows(r), :], qaug)

    def softmax(slot, diagonal):
        for c in range(tb // LANES):
            cs = slice(c * LANES, (c + 1) * LANES)
            u = u_s[slot, :, cs]
            if diagonal:
                kpos = lax.broadcasted_iota(I32, (tb, LANES), 0)
                qpos = lax.broadcasted_iota(I32, (tb, LANES), 1) + c * LANES
                u = jnp.where(kpos <= qpos, u, -jnp.inf)
            m = st_s[0:1, cs]
            m_new = jnp.maximum(m, jnp.max(u, axis=0, keepdims=True))
            alpha = jnp.exp2(m - m_new)
            p = jnp.exp2(u - m_new)
            st_s[0:1, cs] = m_new
            st_s[1:2, cs] = alpha * st_s[1:2, cs] + jnp.sum(p, axis=0, keepdims=True)
            st_s[2:3, cs] = alpha
            p_s[slot, :, cs] = p.astype(BF16)

    def accumulate(slot, r):
        acc_s[...] = st_s[2:3, :] * acc_s[...] + _dot(vt_ref[0, :, key_rows(r)], p_s[slot])

    def step(r, slot, diagonal=False, first=False, last=False):
        if not last:
            scores(r + 1, 1 - slot)
        if not first:
            accumulate(1 - slot, r - 1)
        softmax(slot, diagonal)

    def pair(s, carry):
        step(2 * s + 1, 1)
        step(2 * s + 2, 0)
        return carry

    acc_s[...] = jnp.zeros(acc_s.shape, F32)
    st_s[...] = jnp.concatenate([jnp.full((1, tb), -jnp.inf, F32), jnp.zeros((SUBLANES - 1, tb), F32)], axis=0)
    scores(0, 0)
    step(0, 0, diagonal=True, first=True)
    lax.fori_loop(0, i // 2, pair, 0)

    @pl.when(i % 2 == 1)
    def _():
        step(i, 1, last=True)
        accumulate(1, i)

    @pl.when(i % 2 == 0)
    def _():
        accumulate(0, i)

    o_ref[0] = (acc_s[...] / st_s[1:2, :]).T.astype(o_ref.dtype)


def _fox_prompt(q_t, k_hm, v_t, fcols, *, tb):
    bsz, _, seq_len, _ = k_hm.shape
    kern = functools.partial(_fox_prompt_kernel, tb=tb, prep_rows=min(512, seq_len))
    return pl.pallas_call(
        kern, grid=(bsz, HF, seq_len // tb),
        in_specs=[pl.BlockSpec((1, DHF, tb), lambda b, h, i: (b, h, i)),
                  pl.BlockSpec((1, 1, seq_len, DHF), lambda b, h, i: (b, h, 0, 0)),
                  pl.BlockSpec((1, DHF, seq_len), lambda b, h, i: (b, h, 0)),
                  pl.BlockSpec((seq_len, LANES), lambda b, h, i: (b, 0))],
        out_specs=pl.BlockSpec((1, tb, DHF), lambda b, h, i: (b, i, h)),
        out_shape=jax.ShapeDtypeStruct((bsz, seq_len, FOX_HEAD_COLS), BF16),
        scratch_shapes=[pltpu.VMEM((seq_len, 2 * DHF), BF16), pltpu.VMEM((2, tb, tb), F32),
                        pltpu.VMEM((2, tb, tb), BF16), pltpu.VMEM((DHF, tb), F32),
                        pltpu.VMEM((SUBLANES, tb), F32)],
        compiler_params=_cparams(("arbitrary", "arbitrary", "arbitrary")), name="fox_prompt",
    )(q_t, k_hm, v_t, fcols)


def _fox_sample_kernel(qbd_ref, ck_ref, cv_ref, clf_ref, kn_ref, vn_ref, lfn_ref, o_ref,
                       m_s, l_s, acc_s, f_s, *, kb, n_new):
    j = pl.program_id(1)
    nj = pl.num_programs(1)

    @pl.when(j == 0)
    def _():
        m_s[...] = jnp.full(m_s.shape, -jnp.inf, F32)
        l_s[...] = jnp.zeros(l_s.shape, F32)
        acc_s[...] = jnp.zeros(acc_s.shape, F32)
        f_s[...] = jnp.zeros(f_s.shape, F32)

    qbd = qbd_ref[0]
    eye = (lax.broadcasted_iota(I32, (LANES, LANES), 0) == lax.broadcasted_iota(I32, (LANES, LANES), 1))

    def to_col(row):
        return jnp.sum(jnp.where(eye, row, 0.0), axis=1, keepdims=True)

    def step(kf32, vf32, lf, valid):
        rows = kf32.shape[0]
        ri = lax.broadcasted_iota(I32, (rows, rows), 0)
        ci = lax.broadcasted_iota(I32, (rows, rows), 1)
        tril = (ci <= ri).astype(BF16)
        fk = f_s[...] + sum(_dot(tril, p) for p in _split3(lf))
        f_s[...] = fk[rows - 1:rows, :]
        u = _dot(kf32.astype(BF16), qbd) - fk * LOG2E
        if valid is not None:
            u = jnp.where(valid, u, -jnp.inf)
        m = m_s[...]
        m_new = jnp.maximum(m, jnp.max(u, axis=0, keepdims=True))
        alpha = jnp.exp2(m - m_new)
        p = jnp.exp2(u - m_new)
        l_s[...] = alpha * l_s[...] + jnp.sum(p, axis=0, keepdims=True)
        m_s[...] = m_new
        acc_s[...] = to_col(alpha) * acc_s[...] + _dot_tn(p.astype(BF16), vf32.astype(BF16))

    step(ck_ref[0], cv_ref[0], clf_ref[0], None)

    @pl.when(j == nj - 1)
    def _():
        rows = kn_ref.shape[1]
        r = lax.broadcasted_iota(I32, (rows, LANES), 0)
        t = lax.broadcasted_iota(I32, (rows, LANES), 1) % n_new
        step(kn_ref[0], vn_ref[0], lfn_ref[0], r <= t)
        inv = 1.0 / to_col(l_s[...])
        acc = acc_s[...] * inv
        for h in range(HF):
            o_ref[0, :, h * DHF:(h + 1) * DHF] = acc[h * n_new:(h + 1) * n_new,
                                                     h * DHF:(h + 1) * DHF].astype(o_ref.dtype)


def _fox_sample(qbd, ck, cv, clf_exp, k_new, v_new, lf_new_exp, *, kb, n_new, cache_off):
    bsz = qbd.shape[0]
    past = ck.shape[1]
    rows_new = k_new.shape[1]
    kern = functools.partial(_fox_sample_kernel, kb=kb, n_new=n_new)
    blk = lambda w: pl.BlockSpec((1, kb, w), lambda b, j: (b, j, 0))
    cblk = lambda w: pl.BlockSpec((1, kb, w), lambda b, j: (b + cache_off, j, 0))
    new = lambda w: pl.BlockSpec((1, rows_new, w), lambda b, j: (b, 0, 0))
    return pl.pallas_call(
        kern, grid=(bsz, past // kb),
        in_specs=[pl.BlockSpec((1, FOX_HEAD_COLS, LANES), lambda b, j: (b, 0, 0)),
                  cblk(FOX_HEAD_COLS), cblk(FOX_HEAD_COLS), blk(LANES),
                  new(FOX_HEAD_COLS), new(FOX_HEAD_COLS), new(LANES)],
        out_specs=pl.BlockSpec((1, n_new, FOX_HEAD_COLS), lambda b, j: (b, 0, 0)),
        out_shape=jax.ShapeDtypeStruct((bsz, n_new, FOX_HEAD_COLS), BF16),
        scratch_shapes=[pltpu.VMEM((1, LANES), F32), pltpu.VMEM((1, LANES), F32),
                        pltpu.VMEM((LANES, FOX_HEAD_COLS), F32), pltpu.VMEM((1, LANES), F32)],
        compiler_params=_cparams(("arbitrary", "arbitrary")), name="fox_sample",
    )(qbd, ck, cv, clf_exp, k_new, v_new, lf_new_exp)


def _layer_norm_rows(r, g, b):
    mu = jnp.mean(r, axis=1, keepdims=True)
    rc = r - mu
    var = jnp.mean(rc * rc, axis=1, keepdims=True)
    return rc * lax.rsqrt(var + LN_EPS) * g + b


def _outproj_router_kernel(hmp_ref, ofp_ref, xp_ref, hms_ref, ofs_ref, xs_ref,
                           wo_ref, lg_ref, lb_ref, wrh_ref, wrl_ref, rb_ref,
                           h1_ref, h1p_ref, eidx_ref, gw_ref, rank_ref, cnt_ref, cnt_s, *, ntp, n_valid):
    i = pl.program_id(0)
    tm = xp_ref.shape[0]

    @pl.when(i == 0)
    def _():
        cnt_s[...] = jnp.zeros(cnt_s.shape, F32)

    is_p = i < ntp
    hm = jnp.where(is_p, hmp_ref[...], hms_ref[...])
    of = jnp.where(is_p, ofp_ref[...], ofs_ref[...])
    x = jnp.where(is_p, xp_ref[...], xs_ref[...])
    half = hm.shape[1]
    mix = _dot(hm, wo_ref[0:half, :]) + _dot(of, wo_ref[half:, :])
    h1 = _layer_norm_rows(ALPHA * x + mix, lg_ref[...], lb_ref[...])
    h1_ref[...] = h1
    h1p_ref[...] = _pack_halves(h1)

    hh, hl = _split2(h1)
    logits = _dot_nt(wrh_ref[...], hh) + _dot_nt(wrl_ref[...], hh) + _dot_nt(wrh_ref[...], hl)
    scores = jax.nn.sigmoid(logits)
    biased = scores + rb_ref[...]

    neg = -jnp.inf
    sub = lax.broadcasted_iota(I32, (GROUP_SIZE, tm), 0).astype(F32)
    gs_rows = []
    for gi in range(N_GROUPS):
        grp = biased[gi * GROUP_SIZE:(gi + 1) * GROUP_SIZE, :]
        m1 = jnp.max(grp, axis=0, keepdims=True)
        i1 = jnp.min(jnp.where(grp == m1, sub, float(GROUP_SIZE)), axis=0, keepdims=True)
        m2 = jnp.max(jnp.where(sub == i1, neg, grp), axis=0, keepdims=True)
        gs_rows.append(m1 + m2)
    gsc = jnp.concatenate(gs_rows, axis=0)
    gid = lax.broadcasted_iota(I32, (N_GROUPS, tm), 0).astype(F32)
    gsel = jnp.zeros((N_GROUPS, tm), F32)
    for _ in range(TOP_GROUPS):
        m = jnp.max(gsc, axis=0, keepdims=True)
        idx = jnp.min(jnp.where(gsc == m, gid, float(N_GROUPS)), axis=0, keepdims=True)
        hit = gid == idx
        gsel = jnp.where(hit, 1.0, gsel)
        gsc = jnp.where(hit, neg, gsc)
    emask = jnp.concatenate(
        [jnp.broadcast_to(gsel[gi:gi + 1, :], (GROUP_SIZE, tm)) for gi in range(N_GROUPS)], axis=0)
    masked = jnp.where(emask > 0.0, biased, neg)

    eid = lax.broadcasted_iota(I32, (N_EXPERTS, tm), 0).astype(F32)
    sel = jnp.zeros((N_EXPERTS, tm), F32)
    idx_rows, gw_rows = [], []
    for _ in range(TOP_K):
        m = jnp.max(masked, axis=0, keepdims=True)
        idx = jnp.min(jnp.where(masked == m, eid, float(N_EXPERTS)), axis=0, keepdims=True)
        hit = eid == idx
        gw_rows.append(jnp.sum(jnp.where(hit, scores, 0.0), axis=0, keepdims=True))
        masked = jnp.where(hit, neg, masked)
        sel = jnp.where(hit, 1.0, sel)
        idx_rows.append(idx)
    tok = i * tm + lax.broadcasted_iota(I32, (1, tm), 1)
    sel = jnp.where(tok < n_valid, sel, 0.0)
    gsum = gw_rows[0]
    for r in gw_rows[1:]:
        gsum = gsum + r
    gscale = ROUTE_SCALE / gsum

    ri = lax.broadcasted_iota(I32, (tm, tm), 0)
    ci = lax.broadcasted_iota(I32, (tm, tm), 1)
    before = (ri < ci).astype(BF16)
    rankmat = _dot(sel.astype(BF16), before) + cnt_s[:, 0:1]
    cnt_s[...] = cnt_s[...] + jnp.sum(sel, axis=1, keepdims=True)
    for kk in range(TOP_K):
        hit = eid == idx_rows[kk]
        rk = jnp.sum(jnp.where(hit, rankmat, 0.0), axis=0, keepdims=True)
        eidx_ref[kk:kk + 1, :] = idx_rows[kk].astype(I32)
        gw_ref[kk:kk + 1, :] = gw_rows[kk] * gscale
        rank_ref[kk:kk + 1, :] = rk.astype(I32)

    @pl.when(i == pl.num_programs(0) - 1)
    def _():
        cnt_ref[...] = cnt_s[...]


def _outproj_router(hm_p, of_p, x_p, hm_s, of_s, x_s, w_out, ln_g, ln_b, wr_hi, wr_lo, rbias_col,
                    *, tm, n_valid):
    n_p, d = x_p.shape
    half = hm_p.shape[1]
    ntp = n_p // tm
    n_all = n_p + tm
    prow = lambda w: pl.BlockSpec((tm, w), lambda i: (jnp.minimum(i, ntp - 1), 0))
    srow = lambda w: pl.BlockSpec((tm, w), lambda i: (0, 0))
    orow = lambda w: pl.BlockSpec((tm, w), lambda i: (i, 0))
    trow = pl.BlockSpec((TOP_K, tm), lambda i: (0, i))
    cnt_spec = pl.BlockSpec((N_EXPERTS, LANES), lambda i: (0, 0))
    kern = functools.partial(_outproj_router_kernel, ntp=ntp, n_valid=n_valid)
    return pl.pallas_call(
        kern, grid=(ntp + 1,),
        in_specs=[prow(half), prow(half), prow(d), srow(half), srow(half), srow(d),
                  _const_spec(w_out.shape), _const_spec(ln_g.shape), _const_spec(ln_b.shape),
                  _const_spec(wr_hi.shape), _const_spec(wr_lo.shape), _const_spec(rbias_col.shape)],
        out_specs=[orow(d), orow(d // 2), trow, trow, trow, cnt_spec],
        out_shape=[jax.ShapeDtypeStruct((n_all, d), F32), jax.ShapeDtypeStruct((n_all, d // 2), U32),
                   jax.ShapeDtypeStruct((TOP_K, n_all), I32), jax.ShapeDtypeStruct((TOP_K, n_all), F32),
                   jax.ShapeDtypeStruct((TOP_K, n_all), I32), jax.ShapeDtypeStruct((N_EXPERTS, LANES), F32)],
        scratch_shapes=[pltpu.VMEM((N_EXPERTS, LANES), F32)],
        compiler_params=_cparams(("arbitrary",)), name="outproj_router",
    )(hm_p, of_p, x_p, hm_s, of_s, x_s, w_out, ln_g, ln_b, wr_hi, wr_lo, rbias_col)


def _dispatch_kernel(dest_ref, zflag_ref, h_ref, xs_hbm, zeros_s, sem, zsem, *, tm):
    i = pl.program_id(0)
    nblk = zflag_ref.shape[0]

    def zero_copy(b):
        rows = pl.ds(pl.multiple_of(b * EXPERT_BLOCK, EXPERT_BLOCK), EXPERT_BLOCK)
        return pltpu.make_async_copy(zeros_s, xs_hbm.at[rows, :], zsem)

    @pl.when(i == 0)
    def _():
        zeros_s[...] = jnp.zeros(zeros_s.shape, zeros_s.dtype)

        def start(b, _):
            @pl.when(zflag_ref[b] != 0)
            def _():
                zero_copy(b).start()
            return 0

        def wait(b, _):
            @pl.when(zflag_ref[b] != 0)
            def _():
                zero_copy(b).wait()
            return 0

        lax.fori_loop(0, nblk, start, 0)
        lax.fori_loop(0, nblk, wait, 0)

    def issue(t, _):
        for k in range(TOP_K):
            d = dest_ref[t * TOP_K + k]
            pltpu.make_async_copy(h_ref.at[pl.ds(t, 1), :], xs_hbm.at[pl.ds(d, 1), :], sem).start(priority=k % 2)
        return 0

    lax.fori_loop(0, tm, issue, 0)
    for _ in range(TOP_K):
        pltpu.make_async_copy(h_ref, xs_hbm.at[pl.ds(0, tm), :], sem).wait()


def _dispatch(h1p, dest_flat, zflag, *, tm, rows_alloc):
    n_all, c = h1p.shape
    kern = functools.partial(_dispatch_kernel, tm=tm)
    return pl.pallas_call(
        kern, grid=(n_all // tm,),
        in_specs=[pl.BlockSpec((tm * TOP_K,), lambda i: (i,), memory_space=pltpu.SMEM),
                  pl.BlockSpec(memory_space=pltpu.SMEM),
                  pl.BlockSpec((tm, c), lambda i: (i, 0))],
        out_specs=pl.BlockSpec(memory_space=pl.ANY),
        out_shape=jax.ShapeDtypeStruct((rows_alloc, c), U32),
        scratch_shapes=[pltpu.VMEM((EXPERT_BLOCK, c), U32), pltpu.SemaphoreType.DMA(()),
                        pltpu.SemaphoreType.DMA(())],
        compiler_params=_cparams(("arbitrary",)), name="dispatch",
    )(dest_flat, zflag, h1p)


def _expert_kernel(blk0_ref, nblk_ref, nact_ref, wg_ref, wu_ref, wd_ref, xs_hbm, ys_hbm,
                   wgu_b, wd_b, xbuf, ybuf, in_sem, out_sem, *, nb_total):
    e = pl.program_id(0)
    de = wg_ref.shape[2]
    nact = nact_ref[0]
    b0 = blk0_ref[e]
    nb = nblk_ref[e]

    def rows(g):
        return pl.ds(pl.multiple_of(g * EXPERT_BLOCK, EXPERT_BLOCK), EXPERT_BLOCK)

    def in_copy(g, slot):
        return pltpu.make_async_copy(xs_hbm.at[rows(g), :], xbuf.at[slot], in_sem.at[slot])

    def out_copy(g, slot):
        return pltpu.make_async_copy(ybuf.at[slot], ys_hbm.at[rows(g), :], out_sem.at[slot])

    @pl.when(jnp.logical_and(e == 0, nact > 0))
    def _():
        in_copy(0, 0).start()

    @pl.when(nb > 0)
    def _():
        wgu_b[:, 0:de] = wg_ref[0].astype(BF16)
        wgu_b[:, de:2 * de] = wu_ref[0].astype(BF16)
        wd_b[...] = wd_ref[0].astype(BF16)

    def block(j, _):
        g = b0 + j
        slot = g % 2

        @pl.when(g + 1 < nact)
        def _():
            in_copy(g + 1, 1 - slot).start()

        in_copy(g, slot).wait()

        @pl.when(g >= 2)
        def _():
            out_copy(g - 2, slot).wait()

        x = _unpack_halves(xbuf[slot]).astype(BF16)
        h = _dot(x, wgu_b[...])
        hb = (_silu(h[:, 0:de]) * h[:, de:2 * de]).astype(BF16)
        ybuf[slot] = _pack_halves(_dot(hb, wd_b[...]))
        out_copy(g, slot).start()
        return 0

    lax.fori_loop(0, nb, block, 0)

    @pl.when(e == pl.num_programs(0) - 1)
    def _():
        for back in (2, 1):
            @pl.when(nact >= back)
            def _():
                out_copy(nact - back, (nact - back) % 2).wait()

        ybuf[0] = jnp.zeros(ybuf.shape[1:], ybuf.dtype)

        def zero_start(g, _):
            out_copy(g, 0).start()
            return 0

        def zero_wait(g, _):
            out_copy(g, 0).wait()
            return 0

        lax.fori_loop(nact, nb_total, zero_start, 0)
        lax.fori_loop(nact, nb_total, zero_wait, 0)


def _experts(xs, w_gate, w_up, w_down, blk0, nblk, n_active, *, nb_total, expert_off):
    _, d, de = w_gate.shape
    wspec = lambda shape: pl.BlockSpec((1,) + shape, lambda e, *_: (e + expert_off, 0, 0))
    hbm = pl.BlockSpec(memory_space=pl.ANY)
    grid_spec = pltpu.PrefetchScalarGridSpec(
        num_scalar_prefetch=3, grid=(N_EXPERTS,),
        in_specs=[wspec((d, de)), wspec((d, de)), wspec((de, d)), hbm],
        out_specs=hbm,
        scratch_shapes=[pltpu.VMEM((d, 2 * de), BF16), pltpu.VMEM((de, d), BF16),
                        pltpu.VMEM((2, EXPERT_BLOCK, d // 2), U32), pltpu.VMEM((2, EXPERT_BLOCK, d // 2), U32),
                        pltpu.SemaphoreType.DMA((2,)), pltpu.SemaphoreType.DMA((2,))])
    kern = functools.partial(_expert_kernel, nb_total=nb_total)
    return pl.pallas_call(
        kern, grid_spec=grid_spec,
        out_shape=jax.ShapeDtypeStruct((nb_total * EXPERT_BLOCK, d // 2), U32),
        compiler_params=_cparams(("arbitrary",)), name="experts",
    )(blk0, nblk, n_active, w_gate, w_up, w_down, xs)


def _combine_ln_kernel(src_ref, h1_ref, h1p_ref, gw_ref, ys_hbm, wg_ref, wu_ref, wd_ref, lg_ref, lb_ref,
                       yp_ref, ysm_ref, g_s, sem, *, ntp):
    i = pl.program_id(0)
    tm = h1_ref.shape[0]

    def issue(t, _):
        for k in range(TOP_K):
            r = src_ref[t * TOP_K + k]
            pltpu.make_async_copy(ys_hbm.at[pl.ds(r, 1), :], g_s.at[k, pl.ds(t, 1), :], sem).start(priority=k % 2)
        return 0

    lax.fori_loop(0, tm, issue, 0)
    xb = _unpack_halves(h1p_ref[...]).astype(BF16)
    hb = (_silu(_dot(xb, wg_ref[...])) * _dot(xb, wu_ref[...])).astype(BF16)
    f = _dot(hb, wd_ref[...])
    for k in range(TOP_K):
        pltpu.make_async_copy(ys_hbm.at[pl.ds(0, tm), :], g_s.at[k], sem).wait()
    gw = gw_ref[...]
    for k in range(TOP_K):
        f = f + gw[:, k:k + 1] * _unpack_halves(g_s[k])
    y = _layer_norm_rows(ALPHA * h1_ref[...] + f, lg_ref[...], lb_ref[...])

    @pl.when(i < ntp)
    def _():
        yp_ref[...] = y

    @pl.when(i == ntp)
    def _():
        ysm_ref[...] = y


def _combine_ln(src_flat, h1, h1p, gw, ys, ws_gate, ws_up, ws_down, ln_g, ln_b, *, tm, n_p):
    n_all, d = h1.shape
    ntp = n_p // tm
    kern = functools.partial(_combine_ln_kernel, ntp=ntp)
    return pl.pallas_call(
        kern, grid=(n_all // tm,),
        in_specs=[pl.BlockSpec((tm * TOP_K,), lambda i: (i,), memory_space=pltpu.SMEM),
                  pl.BlockSpec((tm, d), lambda i: (i, 0)), pl.BlockSpec((tm, d // 2), lambda i: (i, 0)),
                  pl.BlockSpec((tm, TOP_K), lambda i: (i, 0)), pl.BlockSpec(memory_space=pl.ANY),
                  _const_spec(ws_gate.shape), _const_spec(ws_up.shape),
                  _const_spec(ws_down.shape), _const_spec(ln_g.shape), _const_spec(ln_b.shape)],
        out_specs=[pl.BlockSpec((tm, d), lambda i: (jnp.minimum(i, ntp - 1), 0)),
                   pl.BlockSpec((tm, d), lambda i: (0, 0))],
        out_shape=[jax.ShapeDtypeStruct((ntp * tm, d), F32), jax.ShapeDtypeStruct((tm, d), F32)],
        scratch_shapes=[pltpu.VMEM((TOP_K, tm, d // 2), U32), pltpu.SemaphoreType.DMA(())],
        compiler_params=_cparams(("arbitrary",)), name="combine_ln",
    )(src_flat, h1, h1p, gw, ys, ws_gate, ws_up, ws_down, ln_g, ln_b)


def _prep_weights(w_in, b_gate, conv_w, conv_b, norm_g, w_out, ln1_g, ln1_b, w_router, router_bias,
                  w_s_gate, w_s_up, w_s_down, ln2_g, ln2_b):
    d = w_in.shape[0]
    n_main = QK_COLS + 2 * VM_COLS + 3 * FOX_HEAD_COLS
    n_gate = w_in.shape[1] - n_main
    w_main = w_in[:, :n_main].astype(BF16)
    wg = jnp.zeros((d, LANES), F32).at[:, :n_gate].set(w_in[:, n_main:])
    bg = jnp.zeros((LANES,), F32).at[:n_gate].set(b_gate)
    for c in FOX_GATE_COPIES[1:]:
        wg = wg.at[:, c:c + HF].set(w_in[:, n_main + 2 * HM:])
        bg = bg.at[c:c + HF].set(b_gate[2 * HM:])
    wg = wg.astype(BF16)
    wr_hi = w_router.T.astype(BF16)
    wr_lo = (w_router.T - wr_hi.astype(F32)).astype(BF16)
    return dict(
        w_main=w_main, w_gate=wg, w_gate_t=wg.T, bg_row=bg[None, :], bg_col=bg[:, None],
        conv_w8=jnp.zeros((SUBLANES, QK_COLS), F32).at[:CONV_W].set(conv_w), conv_b=conv_b[None, :],
        norm_g=norm_g[None, :], w_out=w_out.astype(BF16), ln1_g=ln1_g[None, :], ln1_b=ln1_b[None, :],
        wr_hi=wr_hi, wr_lo=wr_lo, rbias=router_bias[:, None],
        ws_gate=w_s_gate.astype(BF16), ws_up=w_s_up.astype(BF16), ws_down=w_s_down.astype(BF16),
        ln2_g=ln2_g[None, :], ln2_b=ln2_b[None, :])


def _pad_rows(a, bsz, t, tp, value=0.0):
    w = a.shape[-1]
    a = a.reshape(bsz, t, w)
    pad = jnp.full((bsz, tp - t, w), value, a.dtype) if not hasattr(value, "shape") else \
        jnp.broadcast_to(value.astype(a.dtype), (bsz, tp - t, w))
    return jnp.concatenate([a, pad], axis=1).reshape(bsz * tp, w)


def _mixer_prompt(x, wts, *, tm, chunk, tb):
    bsz, t, d = x.shape
    n = bsz * t
    x2 = x.reshape(n, d)
    uqk, vm, om, qt, kf, vf, kb, vt, g, gt = _inproj(
        x2, wts["w_main"], wts["w_gate"], wts["w_gate_t"], wts["bg_row"], wts["bg_col"],
        tm=tm, seq_len=t, prompt=True)
    fcols, lf = _fox_cumsum(g, bsz, t, min(512, t))
    zeros = lambda *s: jnp.zeros(s, F32)
    hm, c1, n1, m1 = _mlstm(uqk, vm, om, g, gt, zeros(bsz, SUBLANES, QK_COLS), wts["conv_w8"], wts["conv_b"],
                            wts["norm_g"], zeros(bsz, HM, DQK, DV), zeros(bsz, HM, 1, DQK),
                            zeros(bsz, HM, 1, LANES), bsz=bsz, seq_len=t, chunk=chunk)
    of = _fox_prompt(qt, kb, vt, fcols, tb=tb).reshape(n, FOX_HEAD_COLS)
    state = (kf.reshape(bsz, t, HF, DHF), vf.reshape(bsz, t, HF, DHF),
             lf[:, 2 * HM:2 * HM + HF].reshape(bsz, t, HF), c1, n1[:, :, 0, :], m1[:, :, 0, 0],
             uqk.reshape(bsz, t, QK_COLS)[:, t - (CONV_W - 1):, :])
    return x2, hm, of, state


def _mixer_sample(x, conv_hist, c0, n0, m0, ck, cv, clogf, wts, *, chunk, kb, cache_off):
    bsz, t, d = x.shape
    n = bsz * t
    past = ck.shape[1]
    x2 = x.reshape(n, d)
    uqk, vm, om, qf, kf, vf, g, gt = _inproj(
        x2, wts["w_main"], wts["w_gate"], wts["w_gate_t"], wts["bg_row"], wts["bg_col"],
        tm=n, seq_len=t, prompt=False)
    gpad = jnp.concatenate([jnp.full((HM,), NEG_BIG, F32), jnp.full((HM,), -NEG_BIG, F32),
                            jnp.zeros((LANES - 2 * HM,), F32)])
    g_p = _pad_rows(g, bsz, t, chunk, gpad)
    hist8 = jnp.concatenate([jnp.zeros((bsz, SUBLANES - (CONV_W - 1), QK_COLS), F32), conv_hist], axis=1)
    hm_p, c1, n1, m1 = _mlstm(
        _pad_rows(uqk, bsz, t, chunk), _pad_rows(vm, bsz, t, chunk), _pad_rows(om, bsz, t, chunk),
        g_p, g_p.T, hist8, wts["conv_w8"], wts["conv_b"], wts["norm_g"],
        c0, n0[:, :, None, :], jnp.broadcast_to(m0[:, :, None, None], (bsz, HM, 1, LANES)),
        bsz=bsz, seq_len=chunk, chunk=chunk)
    hm = hm_p.reshape(bsz, chunk, VM_COLS)[:, :t].reshape(n, VM_COLS)
    lf_new = _log_sigmoid_rows(g[:, 2 * HM:2 * HM + HF])
    q4 = qf.reshape(bsz, t, HF, DHF)
    eye = jnp.eye(HF, dtype=BF16)
    qbd = jnp.einsum("bthd,hg->bhdgt", q4, eye).reshape(bsz, FOX_HEAD_COLS, HF * t)
    rows_new = LANES
    expand = lambda a: jnp.repeat(a, t, axis=-1)
    pad3 = lambda a: jnp.concatenate(
        [a, jnp.zeros((bsz, rows_new - t, a.shape[-1]), a.dtype)], axis=1)
    of = _fox_sample(qbd, ck.reshape(-1, past, FOX_HEAD_COLS), cv.reshape(-1, past, FOX_HEAD_COLS),
                     expand(clogf), pad3(kf.reshape(bsz, t, FOX_HEAD_COLS)),
                     pad3(vf.reshape(bsz, t, FOX_HEAD_COLS)), pad3(expand(lf_new.reshape(bsz, t, HF))),
                     kb=kb, n_new=t, cache_off=cache_off).reshape(n, FOX_HEAD_COLS)
    state = (kf.reshape(bsz, t, HF, DHF), vf.reshape(bsz, t, HF, DHF), lf_new.reshape(bsz, t, HF),
             c1, n1[:, :, 0, :], m1[:, :, 0, 0],
             jnp.concatenate([conv_hist, uqk.reshape(bsz, t, QK_COLS)], axis=1)[:, -(CONV_W - 1):, :])
    return x2, hm, of, state


def _log_sigmoid_rows(a):
    n, w = a.shape
    ap = jnp.zeros((n, LANES), F32).at[:, :w].set(a)

    def kern(a_ref, o_ref):
        o_ref[...] = _log_sigmoid(a_ref[...])

    out = pl.pallas_call(kern, out_shape=jax.ShapeDtypeStruct((n, LANES), F32), name="log_sigmoid")(ap)
    return out[:, :w]


def _dest_rows_kernel(pstart_ref, eidx_ref, rank_ref, o_ref):
    e = eidx_ref[...]

    def add_expert(x, acc):
        return acc + jnp.where(e == x, pstart_ref[x], 0)

    o_ref[...] = lax.fori_loop(0, N_EXPERTS, add_expert, rank_ref[...])


def _dest_rows(pstart, eidx_t, rank_t):
    k, n = eidx_t.shape
    tn = max(c for c in range(LANES, 8192 + 1, LANES) if n % c == 0)
    spec = pl.BlockSpec((k, tn), lambda i: (0, i))
    return pl.pallas_call(
        _dest_rows_kernel, grid=(n // tn,),
        in_specs=[pl.BlockSpec(memory_space=pltpu.SMEM), spec, spec], out_specs=spec,
        out_shape=jax.ShapeDtypeStruct((k, n), I32),
        compiler_params=_cparams(("arbitrary",)), name="dest_rows",
    )(pstart, eidx_t, rank_t)


def _dispatch_tables(counts, n_rows_total):
    blk = EXPERT_BLOCK
    pcounts = (counts + blk - 1) // blk * blk
    pend = jnp.cumsum(pcounts)
    pstart = pend - pcounts
    nb = n_rows_total // blk
    idx = jnp.arange(nb, dtype=I32)
    blk_e = jnp.minimum(jnp.sum(pend[None, :] <= (idx * blk)[:, None], axis=1), N_EXPERTS - 1).astype(I32)
    n_active = (pend[-1] // blk).astype(I32)
    last_of_run = jnp.concatenate([blk_e[1:] != blk_e[:-1], jnp.ones((1,), bool)])
    zflag = ((idx >= n_active - 1) | last_of_run).astype(I32)
    return pstart.astype(I32), (pstart // blk).astype(I32), (pcounts // blk).astype(I32), n_active.reshape(1), zflag


def kernel(x_prompt, x_sample, cache_fox_k, cache_fox_v, cache_fox_logf, state_mlstm_C, state_mlstm_n,
           state_mlstm_m, state_conv, w_in, b_gate, conv_w, conv_b, mlstm_norm_g, w_out, ln1_g, ln1_b,
           w_router, router_bias, w_e_gate, w_e_up, w_e_down, w_s_gate, w_s_up, w_s_down, ln2_g, ln2_b):
    l = 0
    bp, tp, d = x_prompt.shape
    bs, ts, _ = x_sample.shape
    n_p, n_s = bp * tp, bs * ts
    n_tot = n_p + n_s
    wts = _prep_weights(w_in[l], b_gate[l], conv_w[l], conv_b[l], mlstm_norm_g[l], w_out[l], ln1_g[l], ln1_b[l],
                        w_router[l], router_bias[l], w_s_gate[l], w_s_up[l], w_s_down[l], ln2_g[l], ln2_b[l])

    tm_p = min(256, tp)
    xp2, hm_p, of_p, st_p = _mixer_prompt(x_prompt, wts, tm=tm_p, chunk=min(256, tp), tb=min(512, tp))
    xs2, hm_s, of_s, st_s = _mixer_sample(
        x_sample, state_conv[l], state_mlstm_C[l], state_mlstm_n[l], state_mlstm_m[l],
        cache_fox_k.reshape((-1,) + cache_fox_k.shape[2:]), cache_fox_v.reshape((-1,) + cache_fox_v.shape[2:]),
        cache_fox_logf[l], wts, chunk=LANES, kb=min(512, cache_fox_k.shape[2]), cache_off=l * bs)

    tm = tm_p
    tm_r = min(ROUTER_TILE, tp)
    padr = lambda a: jnp.concatenate([a, jnp.zeros((tm_r - n_s, a.shape[1]), a.dtype)], axis=0)
    h1, h1p, eidx_t, gw_t, rank_t, cnt = _outproj_router(
        hm_p, of_p, xp2, padr(hm_s), padr(of_s), padr(xs2),
        wts["w_out"], wts["ln1_g"], wts["ln1_b"], wts["wr_hi"], wts["wr_lo"], wts["rbias"],
        tm=tm_r, n_valid=n_tot)
    n_all = n_p + tm_r
    n_pad = n_all - n_tot
    counts = cnt[:, 0].astype(I32)

    nk = n_tot * TOP_K
    n_blocks = min(N_EXPERTS, nk) + nk // EXPERT_BLOCK
    rows_total = n_blocks * EXPERT_BLOCK
    pstart, blk0, nblk, n_active, zflag = _dispatch_tables(counts, rows_total)
    dest = _dest_rows(pstart, eidx_t, rank_t).T
    tok = jnp.arange(n_all, dtype=I32)[:, None]
    park = rows_total + (tok - n_tot) * TOP_K + jnp.arange(TOP_K, dtype=I32)[None, :]
    real = tok < n_tot
    xs = _dispatch(h1p, jnp.where(real, dest, park).reshape(-1), zflag,
                   tm=tm, rows_alloc=rows_total + n_pad * TOP_K)
    de = w_e_gate.shape[-1]
    ys = _experts(xs, w_e_gate.reshape(-1, d, de), w_e_up.reshape(-1, d, de), w_e_down.reshape(-1, de, d),
                  blk0, nblk, n_active, nb_total=n_blocks, expert_off=l * N_EXPERTS)
    src = jnp.where(real, dest, 0)
    gw_all = jnp.where(real, gw_t.T, 0.0)
    y_p, y_s = _combine_ln(src.reshape(-1), h1, h1p, gw_all, ys, wts["ws_gate"], wts["ws_up"], wts["ws_down"],
                           wts["ln2_g"], wts["ln2_b"], tm=tm, n_p=n_p)
    y_p = y_p.reshape(bp, tp, d)
    y_s = y_s[:n_s].reshape(bs, ts, d)

    stack = lambda s: tuple(a[None] for a in s)
    return (y_p, y_s) + stack(st_p) + stack(st_s)
```
